```python
import math
import jax, jax.numpy as jnp
from jax import lax
import numpy as np

D_MODEL = 1024
BATCH = 16
SEQ = 4096
DEPTH = 4
DEC_BATCH = 4
DEC_SEQ = 4096
PAST_LEN = 128

HEAD_DIM = 64
A_HEADS = 8
B_HEADS = 8
B_KV_HEADS = 2
A_WIDTH = A_HEADS * HEAD_DIM
B_WIDTH = B_HEADS * HEAD_DIM
B_KV_WIDTH = B_KV_HEADS * HEAD_DIM
MIX_WIDTH = A_WIDTH + B_WIDTH
ATTN_IN = 3 * A_WIDTH + B_WIDTH + 2 * B_KV_WIDTH + MIX_WIDTH
DILATED_PATTERNS = ((128, 1), (512, 4), (2048, 16))
ROPE_THETA = 500000.0
ROPE_DIM = HEAD_DIM // 4
AXIAL_THETA = 10000.0
GRID_W = 64
Q_BLOCK = 128
POOL_WINDOWS = (2, 4, 8, 16)
POOL_GROUPS = 4
POOL_WIDTH = D_MODEL
POOL_GROUP_DIM = POOL_WIDTH // POOL_GROUPS
NORM_EPS = 1e-6
NEG_INF = -1e30

kernel_name = 'hybrid_dilated_axial_pool_encoder'


def _rmsnorm(x, g):
    x32 = x.astype(jnp.float32)
    y = x32 * lax.rsqrt(jnp.mean(x32 * x32, axis=-1, keepdims=True) + NORM_EPS)
    return (y * g.astype(jnp.float32)).astype(x.dtype)


def _rope(x, pos, theta):
    r = x.shape[-1]
    inv = theta ** (-jnp.arange(0, r, 2, dtype=jnp.float32) / r)
    ang = pos.astype(jnp.float32)[:, None] * inv[None, :]
    cos = jnp.cos(ang)[:, None, :]
    sin = jnp.sin(ang)[:, None, :]
    x32 = x.astype(jnp.float32)
    x1, x2 = x32[..., : r // 2], x32[..., r // 2:]
    return jnp.concatenate([x1 * cos - x2 * sin, x1 * sin + x2 * cos], axis=-1).astype(x.dtype)


def _partial_rope(x, pos):
    return jnp.concatenate([_rope(x[..., :ROPE_DIM], pos, ROPE_THETA), x[..., ROPE_DIM:]], axis=-1)


def _axial_rope(x, row, col):
    half = HEAD_DIM // 2
    return jnp.concatenate([_rope(x[..., :half], row, AXIAL_THETA),
                            _rope(x[..., half:], col, AXIAL_THETA)], axis=-1)


def _banded_attention(q, k, v, radius):
    n, L, h, dh = q.shape
    blk = radius
    nb = -(-L // blk)
    lp = nb * blk
    qp = jnp.pad(q, ((0, 0), (0, lp - L), (0, 0), (0, 0))).reshape(n, nb, blk, h, dh)
    pad_k = ((0, 0), (blk, lp - L + blk), (0, 0), (0, 0))
    kb = jnp.pad(k, pad_k).reshape(n, nb + 2, blk, h, dh)
    vb = jnp.pad(v, pad_k).reshape(n, nb + 2, blk, h, dh)
    kw = jnp.concatenate([kb[:, :-2], kb[:, 1:-1], kb[:, 2:]], axis=2)
    vw = jnp.concatenate([vb[:, :-2], vb[:, 1:-1], vb[:, 2:]], axis=2)
    qpos = jnp.arange(lp).reshape(nb, blk)
    kpos = jnp.arange(nb)[:, None] * blk - blk + jnp.arange(3 * blk)[None, :]
    kp = kpos[:, None, :]
    valid = (jnp.abs(qpos[:, :, None] - kp) <= radius) & (kp >= 0) & (kp < L)
    s = jnp.einsum('nbqhd,nbkhd->nbhqk', qp, kw).astype(jnp.float32) / math.sqrt(dh)
    s = jnp.where(valid[None, :, None], s, NEG_INF)
    lse = jax.nn.logsumexp(s, axis=-1)
    p = jnp.exp(s - lse[..., None])
    o = jnp.einsum('nbhqk,nbkhd->nbqhd', p.astype(v.dtype), vw)
    o = o.reshape(n, lp, h, dh)[:, :L]
    lse = lse.transpose(0, 1, 3, 2).reshape(n, lp, h)[:, :L]
    return o, lse


def _dilated_window_attention(q, k, v):
    b, s, h, dh = q.shape
    outs, lses = [], []
    for window, dil in DILATED_PATTERNS:
        radius = window // (2 * dil)
        L = s // dil

        def to_res(t):
            return t.reshape(b, L, dil, h, dh).swapaxes(1, 2).reshape(b * dil, L, h, dh)

        o, lse = _banded_attention(to_res(q), to_res(k), to_res(v), radius)
        outs.append(o.reshape(b, dil, L, h, dh).swapaxes(1, 2).reshape(b, s, h, dh))
        lses.append(lse.reshape(b, dil, L, h).swapaxes(1, 2).reshape(b, s, h))
    wts = jax.nn.softmax(jnp.stack(lses), axis=0)
    out = jnp.einsum('pbsh,pbshd->bshd', wts, jnp.stack(outs).astype(jnp.float32))
    return out.astype(q.dtype)


def _blocked_gqa(q, k, v):
    b, s, hq, dh = q.shape
    hkv = k.shape[2]
    g = hq // hkv
    nb = s // Q_BLOCK
    qb = q.reshape(b, nb, Q_BLOCK, hkv, g, dh).transpose(1, 0, 2, 3, 4, 5)

    def one_block(qi):
        sc = jnp.einsum('bqhgd,bshd->bhgqs', qi, k).astype(jnp.float32) / math.sqrt(dh)
        p = jax.nn.softmax(sc, axis=-1)
        return jnp.einsum('bhgqs,bshd->bqhgd', p.astype(v.dtype), v)

    o = lax.map(one_block, qb)
    return o.transpose(1, 0, 2, 3, 4, 5).reshape(b, s, hq, dh)


def _attn_mixer(h, w_in, q_norm, k_norm, w_out):
    b, s, _ = h.shape
    proj = h @ w_in
    cuts = [A_WIDTH, 2 * A_WIDTH, 3 * A_WIDTH, 3 * A_WIDTH + B_WIDTH,
            3 * A_WIDTH + B_WIDTH + B_KV_WIDTH, 3 * A_WIDTH + B_WIDTH + 2 * B_KV_WIDTH]
    qa, ka, va, qb, kb, vb, gate = jnp.split(proj, cuts, axis=-1)
    pos = jnp.arange(s)
    qa = _partial_rope(qa.reshape(b, s, A_HEADS, HEAD_DIM), pos)
    ka = _partial_rope(ka.reshape(b, s, A_HEADS, HEAD_DIM), pos)
    va = va.reshape(b, s, A_HEADS, HEAD_DIM)
    oa = _dilated_window_attention(qa, ka, va).reshape(b, s, A_WIDTH)
    rows = s // GRID_W
    row = jnp.repeat(jnp.arange(rows), GRID_W)
    col = jnp.tile(jnp.arange(GRID_W), rows)
    qb = _axial_rope(_rmsnorm(qb.reshape(b, s, B_HEADS, HEAD_DIM), q_norm), row, col)
    kb = _axial_rope(_rmsnorm(kb.reshape(b, s, B_KV_HEADS, HEAD_DIM), k_norm), row, col)
    vb = vb.reshape(b, s, B_KV_HEADS, HEAD_DIM)
    ob = _blocked_gqa(qb, kb, vb).reshape(b, s, B_WIDTH)
    y = jnp.concatenate([oa, ob], axis=-1) * jax.nn.silu(gate)
    return y @ w_out


def _pool_mixer(h, w_in, w_grp, scale, w_out):
    b, s, _ = h.shape
    u, gate = jnp.split(h @ w_in, 2, axis=-1)
    u32 = u.astype(jnp.float32)
    cs = jnp.concatenate([jnp.zeros((b, 1, POOL_WIDTH), jnp.float32), jnp.cumsum(u32, axis=1)], axis=1)
    cs = cs.reshape(b, s + 1, POOL_GROUPS, POOL_GROUP_DIM)
    half = jnp.array(POOL_WINDOWS, dtype=jnp.int32) // 2
    t = jnp.arange(s, dtype=jnp.int32)[:, None]
    lo = jnp.clip(t - half[None, :], 0, s)
    hi = jnp.clip(t + half[None, :], 0, s)
    gidx = jnp.arange(POOL_GROUPS)[None, :]
    win_sum = cs[:, hi, gidx] - cs[:, lo, gidx]
    cnt = (hi - lo).astype(jnp.float32)[None, :, :, None]
    pooled = win_sum / cnt - u32.reshape(b, s, POOL_GROUPS, POOL_GROUP_DIM)
    mixed = jnp.einsum('bsgc,gcd->bsgd', pooled.astype(h.dtype), w_grp)
    mixed = mixed * scale.reshape(POOL_GROUPS, POOL_GROUP_DIM)
    y = mixed.reshape(b, s, POOL_WIDTH) * jax.nn.silu(gate)
    return y @ w_out


def _trunk(x, c, ada_w, ada_b, pre_norm, post_norm, attn_w_in, attn_q_norm, attn_k_norm,
           attn_w_out, pool_w_in, pool_w_grp, pool_scale, pool_w_out):
    for l in range(DEPTH):
        i = l // 2
        mod = jax.nn.silu(c) @ ada_w[l] + ada_b[l]
        shift, scl, gate = jnp.split(mod[:, None, :], 3, axis=-1)
        h = _rmsnorm(x, pre_norm[l]) * (1.0 + scl) + shift
        if l % 2 == 0:
            m = _attn_mixer(h, attn_w_in[i], attn_q_norm[i], attn_k_norm[i], attn_w_out[i])
        else:
            m = _pool_mixer(h, pool_w_in[i], pool_w_grp[i], pool_scale[i], pool_w_out[i])
        x = x + gate * _rmsnorm(m, post_norm[l])
    return x


def setup_inputs(seed: int = 0) -> dict:
    key = jax.random.key(seed)
    ks = jax.random.split(key, 20)
    n_attn = (DEPTH + 1) // 2
    n_pool = DEPTH // 2
    f32 = jnp.float32
    nrm = lambda k, shp: jax.random.normal(k, shp, f32)
    return {
        'x_prompt': nrm(ks[0], (BATCH, SEQ, D_MODEL)),
        'x_sample': nrm(ks[1], (DEC_BATCH, DEC_SEQ, D_MODEL)),
        'c_prompt': nrm(ks[2], (BATCH, D_MODEL)),
        'c_sample': nrm(ks[3], (DEC_BATCH, D_MODEL)),
        'ada_w': nrm(ks[4], (DEPTH, D_MODEL, 3 * D_MODEL)) * (0.3 * D_MODEL ** -0.5),
        'ada_b': nrm(ks[5], (DEPTH, 3 * D_MODEL)) * 0.01,
        'pre_norm': 1.0 + 0.05 * nrm(ks[6], (DEPTH, D_MODEL)),
        'post_norm': 1.0 + 0.05 * nrm(ks[7], (DEPTH, D_MODEL)),
        'attn_w_in': nrm(ks[8], (n_attn, D_MODEL, ATTN_IN)) * D_MODEL ** -0.5,
        'attn_q_norm': 1.0 + 0.05 * nrm(ks[9], (n_attn, HEAD_DIM)),
        'attn_k_norm': 1.0 + 0.05 * nrm(ks[10], (n_attn, HEAD_DIM)),
        'attn_w_out': nrm(ks[11], (n_attn, MIX_WIDTH, D_MODEL)) * MIX_WIDTH ** -0.5,
        'pool_w_in': nrm(ks[12], (n_pool, D_MODEL, 2 * POOL_WIDTH)) * D_MODEL ** -0.5,
        'pool_w_grp': nrm(ks[13], (n_pool, POOL_GROUPS, POOL_GROUP_DIM, POOL_GROUP_DIM)) * POOL_GROUP_DIM ** -0.5,
        'pool_scale': 1.0 + 0.1 * nrm(ks[14], (n_pool, POOL_WIDTH)),
        'pool_w_out': nrm(ks[15], (n_pool, POOL_WIDTH, D_MODEL)) * POOL_WIDTH ** -0.5,
    }


def reference(x_prompt, x_sample, c_prompt, c_sample, ada_w, ada_b, pre_norm, post_norm,
              attn_w_in, attn_q_norm, attn_k_norm, attn_w_out,
              pool_w_in, pool_w_grp, pool_scale, pool_w_out):
    y_prompt = _trunk(x_prompt, c_prompt, ada_w, ada_b, pre_norm, post_norm, attn_w_in, attn_q_norm,
                      attn_k_norm, attn_w_out, pool_w_in, pool_w_grp, pool_scale, pool_w_out)
    y_sample = _trunk(x_sample, c_sample, ada_w, ada_b, pre_norm, post_norm, attn_w_in, attn_q_norm,
                      attn_k_norm, attn_w_out, pool_w_in, pool_w_grp, pool_scale, pool_w_out)
    return (y_prompt, y_sample)
```

```python
import functools
import math

import jax
import jax.numpy as jnp
from jax import lax
from jax.experimental import pallas as pl
from jax.experimental.pallas import tpu as pltpu

HEAD_DIM = 64
A_HEADS = 8
B_HEADS = 8
B_KV_HEADS = 2
A_WIDTH = A_HEADS * HEAD_DIM
B_WIDTH = B_HEADS * HEAD_DIM
B_KV_WIDTH = B_KV_HEADS * HEAD_DIM
MIX_WIDTH = A_WIDTH + B_WIDTH
DILATED_PATTERNS = ((128, 1), (512, 4), (2048, 16))
ROPE_THETA = 500000.0
ROPE_DIM = HEAD_DIM // 4
AXIAL_THETA = 10000.0
GRID_W = 64
POOL_WINDOWS = (2, 4, 8, 16)
POOL_GROUPS = 4
NORM_EPS = 1e-6
NEG_INF = -1e30

LANES = 128
SUBLANES = 8
VMEM_LIMIT_BYTES = 56 * 1024 * 1024
Q_SCALE = math.log2(math.e) / math.sqrt(HEAD_DIM)

BF16 = jnp.bfloat16
F32 = jnp.float32


def _params(*semantics):
    return pltpu.CompilerParams(dimension_semantics=semantics, vmem_limit_bytes=VMEM_LIMIT_BYTES)


def _dot(a, b):
    return jnp.dot(a, b, preferred_element_type=F32)


def _silu(x):
    return x * (1.0 / (1.0 + jnp.exp(-x)))


def _mod_norm(x, pre_g, scl, shift):
    ms = jnp.mean(x * x, axis=-1, keepdims=True)
    return (x * lax.rsqrt(ms + NORM_EPS) * pre_g) * (1.0 + scl) + shift


def _post_residual(x, m, post_g, gate):
    ms = jnp.mean(m * m, axis=-1, keepdims=True)
    return x + gate * (m * lax.rsqrt(ms + NORM_EPS) * post_g)


def _ada_kernel(c_ref, w_ref, b_ref, o_ref):
    a = _silu(c_ref[...]).astype(BF16)
    o_ref[0] = _dot(a, w_ref[0].astype(BF16)) + b_ref[0]


def _ada_mod(c_all, ada_w, ada_b):
    depth, d, d3 = ada_w.shape
    nb = c_all.shape[0]
    tn = d
    return pl.pallas_call(
        _ada_kernel,
        grid=(depth, d3 // tn),
        in_specs=[pl.BlockSpec((nb, d), lambda l, j: (0, 0)),
                  pl.BlockSpec((1, d, tn), lambda l, j: (l, 0, j)),
                  pl.BlockSpec((1, 1, tn), lambda l, j: (l, 0, j))],
        out_specs=pl.BlockSpec((1, nb, tn), lambda l, j: (l, 0, j)),
        out_shape=jax.ShapeDtypeStruct((depth, nb, d3), F32),
        compiler_params=_params("arbitrary", "arbitrary"),
        name="ada_mod",
    )(c_all, ada_w, ada_b.reshape(depth, 1, d3))


def _rope_tables(pos, rot_dim, theta, lane_in_block):
    h = rot_dim // 2
    f = jnp.where(lane_in_block >= 0, lane_in_block % h, 0)
    inv = theta ** (-(2.0 * f.astype(F32)) / rot_dim)
    ang = pos.astype(F32)[:, None] * inv[None, :]
    active = (lane_in_block >= 0)[None, :]
    lo = (lane_in_block < h)[None, :] & active
    hi = (lane_in_block >= h)[None, :] & active
    cos = jnp.where(active, jnp.cos(ang), 1.0)
    sin = jnp.sin(ang)
    return cos, jnp.where(lo, -sin, 0.0), jnp.where(hi, sin, 0.0)


def _make_tables(s):
    lane = jnp.arange(LANES) % HEAD_DIM
    pos = jnp.arange(s)
    a_tabs = _rope_tables(pos, ROPE_DIM, ROPE_THETA, jnp.where(lane < ROPE_DIM, lane, -1))
    half = HEAD_DIM // 2
    in_blk = lane % half
    row_t = _rope_tables(pos // GRID_W, half, AXIAL_THETA, in_blk)
    col_t = _rope_tables(pos % GRID_W, half, AXIAL_THETA, in_blk)
    first = (lane < half)[None, :]
    b_tabs = tuple(jnp.where(first, r, c) for r, c in zip(row_t, col_t))
    return a_tabs + b_tabs


def _apply_rope(x, cos, s_lo, s_hi, h):
    return x * cos + pltpu.roll(x, LANES - h, 1) * s_lo + pltpu.roll(x, h, 1) * s_hi


def _attn_front_kernel(x_ref, shift_ref, scl_ref, pre_ref, w_ref, gq_ref, gk_ref, bd_ref,
                       ac_ref, alo_ref, ahi_ref, bc_ref, blo_ref, bhi_ref,
                       qa_ref, ka_ref, va_ref, qb_ref, kt_ref, vb_ref, g_ref):
    h = _mod_norm(x_ref[0], pre_ref[...], scl_ref[0], shift_ref[0]).astype(BF16)
    lane = lax.broadcasted_iota(jnp.int32, (1, LANES), 1)
    a_tabs = (ac_ref[...], alo_ref[...], ahi_ref[...])
    b_tabs = (bc_ref[...], blo_ref[...], bhi_ref[...])
    bd = bd_ref[...]

    def head_rms(xc, g):
        sq = xc * xc
        hi = sq.astype(BF16)
        lo = (sq - hi.astype(F32)).astype(BF16)
        ss = _dot(hi, bd) + _dot(lo, bd)
        return xc * lax.rsqrt(ss * (1.0 / HEAD_DIM) + NORM_EPS) * g

    off = 0
    qa = _dot(h, w_ref[:, off:off + A_WIDTH]); off += A_WIDTH
    for c in range(A_WIDTH // LANES):
        col = _apply_rope(qa[:, c * LANES:(c + 1) * LANES], *a_tabs, ROPE_DIM // 2)
        qa_ref[0, :, c * LANES:(c + 1) * LANES] = (col * Q_SCALE).astype(BF16)
    ka = _dot(h, w_ref[:, off:off + A_WIDTH]); off += A_WIDTH
    for c in range(A_WIDTH // LANES):
        col = _apply_rope(ka[:, c * LANES:(c + 1) * LANES], *a_tabs, ROPE_DIM // 2)
        ka_ref[0, :, c * LANES:(c + 1) * LANES] = col.astype(BF16)
    va_ref[0] = _dot(h, w_ref[:, off:off + A_WIDTH]).astype(BF16); off += A_WIDTH

    qb = _dot(h, w_ref[:, off:off + B_WIDTH]); off += B_WIDTH
    for c in range(B_WIDTH // LANES):
        col = head_rms(qb[:, c * LANES:(c + 1) * LANES], gq_ref[...])
        col = _apply_rope(col, *b_tabs, HEAD_DIM // 4)
        qb_ref[0, :, c * LANES:(c + 1) * LANES] = (col * Q_SCALE).astype(BF16)

    kb = _dot(h, w_ref[:, off:off + B_KV_WIDTH]); off += B_KV_WIDTH
    kb = _apply_rope(head_rms(kb, gk_ref[...]), *b_tabs, HEAD_DIM // 4)
    kt = kb.T
    for j in range(B_KV_HEADS):
        kj = kt[j * HEAD_DIM:(j + 1) * HEAD_DIM].astype(BF16)
        kt_ref[0, j] = jnp.concatenate([kj, kj], axis=0)

    vb = _dot(h, w_ref[:, off:off + B_KV_WIDTH]); off += B_KV_WIDTH
    ones_col = jnp.where(lane == HEAD_DIM, 1.0, 0.0)
    for j in range(B_KV_HEADS):
        vj = vb if j == 0 else pltpu.roll(vb, HEAD_DIM, 1)
        vb_ref[0, j] = jnp.where(lane < HEAD_DIM, vj, ones_col).astype(BF16)

    g_ref[0] = _silu(_dot(h, w_ref[:, off:off + MIX_WIDTH])).astype(BF16)


def _attn_front(x, shift, scl, pre_g, w_in, gq, gk, bd, tabs, tm):
    b, s, d = x.shape
    n_in = w_in.shape[1]
    row = lambda i, j: (i, j, 0)
    per_b = lambda i, j: (i, 0, 0)
    const = lambda i, j: (0, 0)
    tab_spec = pl.BlockSpec((tm, LANES), lambda i, j: (j, 0))
    wide = lambda w: pl.BlockSpec((1, tm, w), row)
    return pl.pallas_call(
        _attn_front_kernel,
        grid=(b, s // tm),
        in_specs=[wide(d), pl.BlockSpec((1, 1, d), per_b), pl.BlockSpec((1, 1, d), per_b),
                  pl.BlockSpec((1, d), const), pl.BlockSpec((d, n_in), const),
                  pl.BlockSpec((1, LANES), const), pl.BlockSpec((1, LANES), const),
                  pl.BlockSpec((LANES, LANES), const)] + [tab_spec] * 6,
        out_specs=[wide(A_WIDTH), wide(A_WIDTH), wide(A_WIDTH), wide(B_WIDTH),
                   pl.BlockSpec((1, B_KV_HEADS, LANES, tm), lambda i, j: (i, 0, 0, j)),
                   pl.BlockSpec((1, B_KV_HEADS, tm, LANES), lambda i, j: (i, 0, j, 0)),
                   wide(MIX_WIDTH)],
        out_shape=[jax.ShapeDtypeStruct((b, s, A_WIDTH), BF16)] * 3
        + [jax.ShapeDtypeStruct((b, s, B_WIDTH), BF16),
           jax.ShapeDtypeStruct((b, B_KV_HEADS, LANES, s), BF16),
           jax.ShapeDtypeStruct((b, B_KV_HEADS, s, LANES), BF16),
           jax.ShapeDtypeStruct((b, s, MIX_WIDTH), BF16)],
        compiler_params=_params("parallel", "parallel"),
        name="attn_front",
    )(x, shift, scl, pre_g, w_in, gq, gk, bd, *tabs)


def _gqa_kernel(q_ref, kt_ref, v_ref, o_ref, *, tk):
    tq = q_ref.shape[1]
    s = kt_ref.shape[3]
    lane = lax.broadcasted_iota(jnp.int32, (1, LANES), 1)
    first = lane < HEAD_DIM
    zero = jnp.zeros((), BF16)
    parts = []
    for c in range(2):
        qc = q_ref[0, :, c * LANES:(c + 1) * LANES]
        parts += [jnp.where(first, qc, zero), jnp.where(first, zero, qc)]
    qs = jnp.concatenate(parts, axis=0)
    rows = qs.shape[0]

    def step(i, carry):
        m, acc = carry
        k0 = pl.multiple_of(i * tk, tk)
        sc = _dot(qs, kt_ref[0, 0, :, pl.ds(k0, tk)])
        m_new = jnp.maximum(m, jnp.max(sc, axis=1, keepdims=True))
        p = jnp.exp2(sc - m_new).astype(BF16)
        acc = acc * jnp.exp2(m - m_new) + _dot(p, v_ref[0, 0, pl.ds(k0, tk), :])
        return m_new, acc

    m0 = jnp.full((rows, 1), NEG_INF, F32)
    acc0 = jnp.zeros((rows, LANES), F32)
    _, acc = lax.fori_loop(0, s // tk, step, (m0, acc0))
    out = acc / acc[:, HEAD_DIM:HEAD_DIM + 1]
    for c in range(2):
        even = out[(2 * c) * tq:(2 * c + 1) * tq]
        odd = pltpu.roll(out[(2 * c + 1) * tq:(2 * c + 2) * tq], HEAD_DIM, 1)
        o_ref[0, :, c * LANES:(c + 1) * LANES] = jnp.where(first, even, odd).astype(BF16)


def _gqa_flash(qb, kt, vb, tq, tk):
    b, s, _ = qb.shape
    group_w = B_WIDTH // B_KV_HEADS
    return pl.pallas_call(
        functools.partial(_gqa_kernel, tk=tk),
        grid=(b, B_KV_HEADS, s // tq),
        in_specs=[pl.BlockSpec((1, tq, group_w), lambda i, j, t: (i, t, j)),
                  pl.BlockSpec((1, 1, LANES, s), lambda i, j, t: (i, j, 0, 0)),
                  pl.BlockSpec((1, 1, s, LANES), lambda i, j, t: (i, j, 0, 0))],
        out_specs=pl.BlockSpec((1, tq, group_w), lambda i, j, t: (i, t, j)),
        out_shape=jax.ShapeDtypeStruct((b, s, B_WIDTH), BF16),
        compiler_params=_params("parallel", "parallel", "arbitrary"),
        name="gqa_flash",
    )(qb, kt, vb)


A_CHUNK = 128
A_RADIUS = 64


def _dilated_kernel(q_ref, k_ref, v_ref, o_ref, stage, q0d, q1d, kd, vd, ores, lres, onat, lnat):
    s = q_ref.shape[1]
    n_chunks = s // A_CHUNK
    kw = 2 * A_CHUNK
    lane = lax.broadcasted_iota(jnp.int32, (1, LANES), 1)
    first = lane < HEAD_DIM

    tail = kw - A_RADIUS
    for buf in (kd, vd):
        buf[0:A_RADIUS, :] = jnp.zeros((A_RADIUS, LANES), BF16)
        buf[A_RADIUS + s:A_RADIUS + s + tail, :] = jnp.zeros((tail, LANES), BF16)

    ri = lax.broadcasted_iota(jnp.int32, (A_CHUNK, kw), 0)
    ci = lax.broadcasted_iota(jnp.int32, (A_CHUNK, kw), 1)
    band1 = (ci >= ri) & (ci <= ri + 2 * A_RADIUS)
    band = jnp.concatenate([band1, band1], axis=0)
    col = lax.broadcasted_iota(jnp.int32, (1, kw), 1)

    for p, (window, dil) in enumerate(DILATED_PATTERNS):
        assert window // (2 * dil) == A_RADIUS
        cls_len = s // dil

        def put(dst, r, val, pad):
            dst[pad + r * cls_len:pad + (r + 1) * cls_len, :] = val

        if dil == 1:
            qv = q_ref[0]
            q0d[...] = jnp.where(first, qv, jnp.zeros((), BF16))
            q1d[...] = jnp.where(first, jnp.zeros((), BF16), qv)
            kd[A_RADIUS:A_RADIUS + s, :] = k_ref[0]
            vd[A_RADIUS:A_RADIUS + s, :] = v_ref[0]
        else:
            stage[...] = q_ref[0].astype(F32)
            for r in range(dil):
                qv = stage[pl.ds(r, cls_len, stride=dil), :]
                put(q0d, r, jnp.where(first, qv, 0.0).astype(BF16), 0)
                put(q1d, r, jnp.where(first, 0.0, qv).astype(BF16), 0)
            stage[...] = k_ref[0].astype(F32)
            for r in range(dil):
                put(kd, r, stage[pl.ds(r, cls_len, stride=dil), :].astype(BF16), A_RADIUS)
            stage[...] = v_ref[0].astype(F32)
            for r in range(dil):
                put(vd, r, stage[pl.ds(r, cls_len, stride=dil), :].astype(BF16), A_RADIUS)

        o_dst, l_dst = (onat.at[p], lnat.at[p]) if dil == 1 else (ores, lres)

        def chunk(c, carry, cls_len=cls_len, o_dst=o_dst, l_dst=l_dst):
            r0 = pl.multiple_of(c * A_CHUNK, A_CHUNK)
            qq = jnp.concatenate([q0d[pl.ds(r0, A_CHUNK), :], q1d[pl.ds(r0, A_CHUNK), :]], axis=0)
            kwin = kd[pl.ds(r0, kw), :]
            vwin = vd[pl.ds(r0, kw), :]
            sc = lax.dot_general(qq, kwin, (((1,), (1,)), ((), ())), preferred_element_type=F32)
            lo = (r0 // cls_len) * cls_len + A_RADIUS - r0
            ok = band & (col >= lo) & (col < lo + cls_len)
            sc = jnp.where(ok, sc, NEG_INF)
            m = jnp.max(sc, axis=1, keepdims=True)
            pr = jnp.exp2(sc - m)
            l = jnp.sum(pr, axis=1, keepdims=True)
            pv = _dot(pr.astype(BF16), vwin)
            o2 = pv / l
            lse2 = jnp.broadcast_to(m + jnp.log2(l), (2 * A_CHUNK, LANES))
            o_dst[pl.ds(r0, A_CHUNK), :] = jnp.where(first, o2[:A_CHUNK], o2[A_CHUNK:])
            l_dst[pl.ds(r0, A_CHUNK), :] = jnp.where(first, lse2[:A_CHUNK], lse2[A_CHUNK:])
            return carry

        lax.fori_loop(0, n_chunks, chunk, 0, unroll=2)

        if dil != 1:
            for r in range(dil):
                onat[p, pl.ds(r, cls_len, stride=dil), :] = ores[r * cls_len:(r + 1) * cls_len, :]
                lnat[p, pl.ds(r, cls_len, stride=dil), :] = lres[r * cls_len:(r + 1) * cls_len, :]

    n_pat = len(DILATED_PATTERNS)
    blk = 256

    def mix(i, carry):
        r0 = pl.multiple_of(i * blk, blk)
        ls = [lnat[p, pl.ds(r0, blk), :] for p in range(n_pat)]
        mx = functools.reduce(jnp.maximum, ls)
        ws = [jnp.exp2(l - mx) for l in ls]
        num = sum(w * onat[p, pl.ds(r0, blk), :] for p, w in enumerate(ws))
        o_ref[0, pl.ds(r0, blk), :] = (num / sum(ws)).astype(BF16)
        return carry

    lax.fori_loop(0, s // blk, mix, 0)


def _dilated_attn(qa, ka, va):
    b, s, w = qa.shape
    n_pat = len(DILATED_PATTERNS)
    spec = pl.BlockSpec((1, s, LANES), lambda i, j: (i, 0, j))
    pad_rows = s + 2 * A_CHUNK
    return pl.pallas_call(
        _dilated_kernel,
        grid=(b, w // LANES),
        in_specs=[spec, spec, spec],
        out_specs=spec,
        out_shape=jax.ShapeDtypeStruct((b, s, w), BF16),
        scratch_shapes=[pltpu.VMEM((s, LANES), F32),
                        pltpu.VMEM((s, LANES), BF16),
                        pltpu.VMEM((s, LANES), BF16),
                        pltpu.VMEM((pad_rows, LANES), BF16),
                        pltpu.VMEM((pad_rows, LANES), BF16),
                        pltpu.VMEM((s, LANES), F32),
                        pltpu.VMEM((s, LANES), F32),
                        pltpu.VMEM((n_pat, s, LANES), F32),
                        pltpu.VMEM((n_pat, s, LANES), F32)],
        compiler_params=_params("parallel", "parallel"),
        name="dilated_attn",
    )(qa, ka, va)


def _attn_back_kernel(x_ref, oa_ref, ob_ref, g_ref, gate_ref, post_ref, w_ref, o_ref):
    ya = (oa_ref[0].astype(F32) * g_ref[0, :, :A_WIDTH].astype(F32)).astype(BF16)
    yb = (ob_ref[0].astype(F32) * g_ref[0, :, A_WIDTH:].astype(F32)).astype(BF16)
    m = _dot(ya, w_ref[:A_WIDTH, :]) + _dot(yb, w_ref[A_WIDTH:, :])
    o_ref[0] = _post_residual(x_ref[0], m, post_ref[...], gate_ref[0])


def _attn_back(x, oa, ob, g, gate, post_g, w_out, tm):
    b, s, d = x.shape
    row = lambda i, j: (i, j, 0)
    per_b = lambda i, j: (i, 0, 0)
    const = lambda i, j: (0, 0)
    return pl.pallas_call(
        _attn_back_kernel,
        grid=(b, s // tm),
        in_specs=[pl.BlockSpec((1, tm, d), row), pl.BlockSpec((1, tm, A_WIDTH), row),
                  pl.BlockSpec((1, tm, B_WIDTH), row), pl.BlockSpec((1, tm, MIX_WIDTH), row),
                  pl.BlockSpec((1, 1, d), per_b), pl.BlockSpec((1, d), const),
                  pl.BlockSpec((MIX_WIDTH, d), const)],
        out_specs=pl.BlockSpec((1, tm, d), row),
        out_shape=jax.ShapeDtypeStruct((b, s, d), F32),
        compiler_params=_params("parallel", "parallel"),
        name="attn_back",
    )(x, oa, ob, g, gate, post_g, w_out)


POOL_HALO = SUBLANES


def _pool_kernel(xp_ref, x_ref, xn_ref, shift_ref, scl_ref, gate_ref, pre_ref, post_ref,
                 w_in_ref, w_grp_ref, scale_ref, w_out_ref, o_ref, u_scr, *, seq):
    tm = x_ref.shape[1]
    width = scale_ref.shape[1]
    gdim = width // POOL_GROUPS
    i = pl.program_id(1)
    x = x_ref[0]
    xe = jnp.concatenate([xp_ref[0], x, xn_ref[0]], axis=0)
    he = _mod_norm(xe, pre_ref[...], scl_ref[0], shift_ref[0]).astype(BF16)
    t_ext = i * tm - POOL_HALO + lax.broadcasted_iota(jnp.int32, (tm + 2 * POOL_HALO, 1), 0)
    u_ext = _dot(he, w_in_ref[:, :width])
    u_scr[...] = jnp.where((t_ext >= 0) & (t_ext < seq), u_ext, 0.0)
    gate_act = _silu(_dot(he[POOL_HALO:POOL_HALO + tm], w_in_ref[:, width:]))
    t = i * tm + lax.broadcasted_iota(jnp.int32, (tm, 1), 0)
    m = jnp.zeros((tm, o_ref.shape[2]), F32)
    for g, window in enumerate(POOL_WINDOWS):
        half = window // 2
        assert half <= POOL_HALO
        cols = slice(g * gdim, (g + 1) * gdim)
        win = u_scr[POOL_HALO - half:POOL_HALO - half + tm, cols]
        for dd in range(-half + 1, half):
            win = win + u_scr[POOL_HALO + dd:POOL_HALO + dd + tm, cols]
        cnt = (jnp.minimum(t + half, seq) - jnp.maximum(t - half, 0)).astype(F32)
        pooled = win / cnt - u_scr[POOL_HALO:POOL_HALO + tm, cols]
        mixed = _dot(pooled.astype(BF16), w_grp_ref[g]) * scale_ref[:, cols]
        y = (mixed * gate_act[:, cols]).astype(BF16)
        m = m + _dot(y, w_out_ref[cols, :])
    o_ref[0] = _post_residual(x, m, post_ref[...], gate_ref[0])


def _pool_layer(x, shift, scl, gate, pre_g, post_g, w_in, w_grp, scale, w_out, tm):
    b, s, d = x.shape
    width = scale.shape[1]
    nt = tm // POOL_HALO
    last = s // POOL_HALO - 1
    row = lambda i, j: (i, j, 0)
    per_b = lambda i, j: (i, 0, 0)
    const = lambda i, j: (0, 0)
    return pl.pallas_call(
        functools.partial(_pool_kernel, seq=s),
        grid=(b, s // tm),
        in_specs=[pl.BlockSpec((1, POOL_HALO, d), lambda i, j: (i, jnp.maximum(j * nt - 1, 0), 0)),
                  pl.BlockSpec((1, tm, d), row),
                  pl.BlockSpec((1, POOL_HALO, d), lambda i, j: (i, jnp.minimum((j + 1) * nt, last), 0)),
                  pl.BlockSpec((1, 1, d), per_b), pl.BlockSpec((1, 1, d), per_b),
                  pl.BlockSpec((1, 1, d), per_b), pl.BlockSpec((1, d), const),
                  pl.BlockSpec((1, d), const), pl.BlockSpec((d, 2 * width), const),
                  pl.BlockSpec(w_grp.shape, lambda i, j: (0, 0, 0)),
                  pl.BlockSpec((1, width), const), pl.BlockSpec((width, d), const)],
        out_specs=pl.BlockSpec((1, tm, d), row),
        out_shape=jax.ShapeDtypeStruct((b, s, d), F32),
        scratch_shapes=[pltpu.VMEM((tm + 2 * POOL_HALO, width), F32)],
        compiler_params=_params("parallel", "parallel"),
        name="pool_layer",
    )(x, x, x, shift, scl, gate, pre_g, post_g, w_in, w_grp, scale, w_out)


def _trunk(x, mods, pre_norm, post_norm, attn_w_in, attn_q_norm, attn_k_norm, attn_w_out,
           pool_w_in, pool_w_grp, pool_scale, pool_w_out, tabs, bd, *, tm, tq, tk):
    depth = pre_norm.shape[0]
    d = x.shape[-1]
    for l in range(depth):
        i = l // 2
        shift, scl, gate = (mods[l][:, None, j * d:(j + 1) * d] for j in range(3))
        pre_g = pre_norm[l][None, :]
        post_g = post_norm[l][None, :]
        if l % 2 == 0:
            gq = jnp.tile(attn_q_norm[i], LANES // HEAD_DIM)[None, :]
            gk = jnp.tile(attn_k_norm[i], LANES // HEAD_DIM)[None, :]
            qa, ka, va, qb, kt, vb, g = _attn_front(x, shift, scl, pre_g, attn_w_in[i], gq, gk, bd,
                                                    tabs, tm)
            oa = _dilated_attn(qa, ka, va)
            ob = _gqa_flash(qb, kt, vb, tq, tk)
            x = _attn_back(x, oa, ob, g, gate, post_g, attn_w_out[i], tm)
        else:
            x = _pool_layer(x, shift, scl, gate, pre_g, post_g, pool_w_in[i], pool_w_grp[i],
                            pool_scale[i][None, :], pool_w_out[i], tm)
    return x


def kernel(x_prompt, x_sample, c_prompt, c_sample, ada_w, ada_b, pre_norm, post_norm, attn_w_in, attn_q_norm, attn_k_norm, attn_w_out, pool_w_in, pool_w_grp, pool_scale, pool_w_out):
    nb_p = x_prompt.shape[0]
    mods = _ada_mod(jnp.concatenate([c_prompt, c_sample], axis=0), ada_w, ada_b)
    head_of_lane = jnp.arange(LANES) // HEAD_DIM
    bd = (head_of_lane[:, None] == head_of_lane[None, :]).astype(BF16)
    weights = (pre_norm, post_norm, attn_w_in.astype(BF16), attn_q_norm, attn_k_norm,
               attn_w_out.astype(BF16), pool_w_in.astype(BF16), pool_w_grp.astype(BF16), pool_scale,
               pool_w_out.astype(BF16))
    outs = []
    for x, sl in ((x_prompt, slice(0, nb_p)), (x_sample, slice(nb_p, None))):
        tabs = _make_tables(x.shape[1])
        outs.append(_trunk(x, mods[:, sl], *weights, tabs, bd, tm=512, tq=256, tk=512))
    return tuple(outs)
```

```python
import functools
import math

import jax
import jax.numpy as jnp
from jax import lax
from jax.experimental import pallas as pl
from jax.experimental.pallas import tpu as pltpu

HEAD_DIM = 64
A_HEADS = 8
B_HEADS = 8
B_KV_HEADS = 2
A_WIDTH = A_HEADS * HEAD_DIM
B_WIDTH = B_HEADS * HEAD_DIM
B_KV_WIDTH = B_KV_HEADS * HEAD_DIM
MIX_WIDTH = A_WIDTH + B_WIDTH
DILATED_PATTERNS = ((128, 1), (512, 4), (2048, 16))
ROPE_THETA = 500000.0
ROPE_DIM = HEAD_DIM // 4
AXIAL_THETA = 10000.0
GRID_W = 64
POOL_WINDOWS = (2, 4, 8, 16)
POOL_GROUPS = 4
NORM_EPS = 1e-6
NEG_INF = -1e30

LANES = 128
SUBLANES = 8
VMEM_LIMIT_BYTES = 56 * 1024 * 1024
Q_SCALE = math.log2(math.e) / math.sqrt(HEAD_DIM)

BF16 = jnp.bfloat16
F32 = jnp.float32


def _params(*semantics):
    return pltpu.CompilerParams(dimension_semantics=semantics, vmem_limit_bytes=VMEM_LIMIT_BYTES)


def _dot(a, b):
    return jnp.dot(a, b, preferred_element_type=F32)


def _silu(x):
    return x * (1.0 / (1.0 + jnp.exp(-x)))


def _mod_norm(x, pre_g, scl, shift):
    ms = jnp.mean(x * x, axis=-1, keepdims=True)
    return (x * lax.rsqrt(ms + NORM_EPS) * pre_g) * (1.0 + scl) + shift


def _post_residual(x, m, post_g, gate):
    ms = jnp.mean(m * m, axis=-1, keepdims=True)
    return x + gate * (m * lax.rsqrt(ms + NORM_EPS) * post_g)


def _ada_kernel(c_ref, w_ref, b_ref, o_ref):
    a = _silu(c_ref[...]).astype(BF16)
    o_ref[0] = _dot(a, w_ref[0].astype(BF16)) + b_ref[0]


def _ada_mod(c_all, ada_w, ada_b):
    depth, d, d3 = ada_w.shape
    nb = c_all.shape[0]
    tn = d
    return pl.pallas_call(
        _ada_kernel,
        grid=(depth, d3 // tn),
        in_specs=[pl.BlockSpec((nb, d), lambda l, j: (0, 0)),
                  pl.BlockSpec((1, d, tn), lambda l, j: (l, 0, j)),
                  pl.BlockSpec((1, 1, tn), lambda l, j: (l, 0, j))],
        out_specs=pl.BlockSpec((1, nb, tn), lambda l, j: (l, 0, j)),
        out_shape=jax.ShapeDtypeStruct((depth, nb, d3), F32),
        compiler_params=_params("arbitrary", "arbitrary"),
        name="ada_mod",
    )(c_all, ada_w, ada_b.reshape(depth, 1, d3))


def _rope_tables(pos, rot_dim, theta, lane_in_block):
    h = rot_dim // 2
    f = jnp.where(lane_in_block >= 0, lane_in_block % h, 0)
    inv = theta ** (-(2.0 * f.astype(F32)) / rot_dim)
    ang = pos.astype(F32)[:, None] * inv[None, :]
    active = (lane_in_block >= 0)[None, :]
    lo = (lane_in_block < h)[None, :] & active
    hi = (lane_in_block >= h)[None, :] & active
    cos = jnp.where(active, jnp.cos(ang), 1.0)
    sin = jnp.sin(ang)
    return cos, jnp.where(lo, -sin, 0.0), jnp.where(hi, sin, 0.0)


def _make_tables(s):
    lane = jnp.arange(LANES) % HEAD_DIM
    pos = jnp.arange(s)
    a_tabs = _rope_tables(pos, ROPE_DIM, ROPE_THETA, jnp.where(lane < ROPE_DIM, lane, -1))
    half = HEAD_DIM // 2
    in_blk = lane % half
    row_t = _rope_tables(pos // GRID_W, half, AXIAL_THETA, in_blk)
    col_t = _rope_tables(pos % GRID_W, half, AXIAL_THETA, in_blk)
    first = (lane < half)[None, :]
    b_tabs = tuple(jnp.where(first, r, c) for r, c in zip(row_t, col_t))
    return a_tabs + b_tabs


def _apply_rope(x, cos, s_lo, s_hi, h):
    return x * cos + pltpu.roll(x, LANES - h, 1) * s_lo + pltpu.roll(x, h, 1) * s_hi


def _attn_front_kernel(x_ref, shift_ref, scl_ref, pre_ref, w_ref, gq_ref, gk_ref, bd_ref,
                       ac_ref, alo_ref, ahi_ref, bc_ref, blo_ref, bhi_ref,
                       qa_ref, ka_ref, va_ref, qb_ref, kt_ref, vb_ref, g_ref):
    h = _mod_norm(x_ref[0], pre_ref[...], scl_ref[0], shift_ref[0]).astype(BF16)
    lane = lax.broadcasted_iota(jnp.int32, (1, LANES), 1)
    a_tabs = (ac_ref[...], alo_ref[...], ahi_ref[...])
    b_tabs = (bc_ref[...], blo_ref[...], bhi_ref[...])
    bd = bd_ref[...]

    def head_rms(xc, g):
        sq = xc * xc
        hi = sq.astype(BF16)
        lo = (sq - hi.astype(F32)).astype(BF16)
        ss = _dot(hi, bd) + _dot(lo, bd)
        return xc * lax.rsqrt(ss * (1.0 / HEAD_DIM) + NORM_EPS) * g

    off = 0
    qa = _dot(h, w_ref[:, off:off + A_WIDTH]); off += A_WIDTH
    for c in range(A_WIDTH // LANES):
        col = _apply_rope(qa[:, c * LANES:(c + 1) * LANES], *a_tabs, ROPE_DIM // 2)
        qa_ref[0, :, c * LANES:(c + 1) * LANES] = (col * Q_SCALE).astype(BF16)
    ka = _dot(h, w_ref[:, off:off + A_WIDTH]); off += A_WIDTH
    for c in range(A_WIDTH // LANES):
        col = _apply_rope(ka[:, c * LANES:(c + 1) * LANES], *a_tabs, ROPE_DIM // 2)
        ka_ref[0, :, c * LANES:(c + 1) * LANES] = col.astype(BF16)
    va_ref[0] = _dot(h, w_ref[:, off:off + A_WIDTH]).astype(BF16); off += A_WIDTH

    qb = _dot(h, w_ref[:, off:off + B_WIDTH]); off += B_WIDTH
    for c in range(B_WIDTH // LANES):
        col = head_rms(qb[:, c * LANES:(c + 1) * LANES], gq_ref[...])
        col = _apply_rope(col, *b_tabs, HEAD_DIM // 4)
        qb_ref[0, :, c * LANES:(c + 1) * LANES] = (col * Q_SCALE).astype(BF16)

    kb = _dot(h, w_ref[:, off:off + B_KV_WIDTH]); off += B_KV_WIDTH
    kb = _apply_rope(head_rms(kb, gk_ref[...]), *b_tabs, HEAD_DIM // 4)
    kt = kb.T
    for j in range(B_KV_HEADS):
        kj = kt[j * HEAD_DIM:(j + 1) * HEAD_DIM].astype(BF16)
        kt_ref[0, j] = jnp.concatenate([kj, kj], axis=0)

    vb = _dot(h, w_ref[:, off:off + B_KV_WIDTH]); off += B_KV_WIDTH
    ones_col = jnp.where(lane == HEAD_DIM, 1.0, 0.0)
    for j in range(B_KV_HEADS):
        vj = vb if j == 0 else pltpu.roll(vb, HEAD_DIM, 1)
        vb_ref[0, j] = jnp.where(lane < HEAD_DIM, vj, ones_col).astype(BF16)

    g_ref[0] = _silu(_dot(h, w_ref[:, off:off + MIX_WIDTH])).astype(BF16)


def _attn_front(x, shift, scl, pre_g, w_in, gq, gk, bd, tabs, tm):
    b, s, d = x.shape
    n_in = w_in.shape[1]
    row = lambda i, j: (i, j, 0)
    per_b = lambda i, j: (i, 0, 0)
    const = lambda i, j: (0, 0)
    tab_spec = pl.BlockSpec((tm, LANES), lambda i, j: (j, 0))
    wide = lambda w: pl.BlockSpec((1, tm, w), row)
    return pl.pallas_call(
        _attn_front_kernel,
        grid=(b, s // tm),
        in_specs=[wide(d), pl.BlockSpec((1, 1, d), per_b), pl.BlockSpec((1, 1, d), per_b),
                  pl.BlockSpec((1, d), const), pl.BlockSpec((d, n_in), const),
                  pl.BlockSpec((1, LANES), const), pl.BlockSpec((1, LANES), const),
                  pl.BlockSpec((LANES, LANES), const)] + [tab_spec] * 6,
        out_specs=[wide(A_WIDTH), wide(A_WIDTH), wide(A_WIDTH), wide(B_WIDTH),
                   pl.BlockSpec((1, B_KV_HEADS, LANES, tm), lambda i, j: (i, 0, 0, j)),
                   pl.BlockSpec((1, B_KV_HEADS, tm, LANES), lambda i, j: (i, 0, j, 0)),
                   wide(MIX_WIDTH)],
        out_shape=[jax.ShapeDtypeStruct((b, s, A_WIDTH), BF16)] * 3
        + [jax.ShapeDtypeStruct((b, s, B_WIDTH), BF16),
           jax.ShapeDtypeStruct((b, B_KV_HEADS, LANES, s), BF16),
           jax.ShapeDtypeStruct((b, B_KV_HEADS, s, LANES), BF16),
           jax.ShapeDtypeStruct((b, s, MIX_WIDTH), BF16)],
        compiler_params=_params("parallel", "parallel"),
        name="attn_front",
    )(x, shift, scl, pre_g, w_in, gq, gk, bd, *tabs)


def _gqa_kernel(q_ref, kt_ref, v_ref, o_ref, *, tk):
    tq = q_ref.shape[1]
    s = kt_ref.shape[3]
    lane = lax.broadcasted_iota(jnp.int32, (1, LANES), 1)
    first = lane < HEAD_DIM
    zero = jnp.zeros((), BF16)
    parts = []
    for c in range(2):
        qc = q_ref[0, :, c * LANES:(c + 1) * LANES]
        parts += [jnp.where(first, qc, zero), jnp.where(first, zero, qc)]
    qs = jnp.concatenate(parts, axis=0)
    rows = qs.shape[0]

    def step(i, carry):
        m, acc = carry
        k0 = pl.multiple_of(i * tk, tk)
        sc = _dot(qs, kt_ref[0, 0, :, pl.ds(k0, tk)])
        m_new = jnp.maximum(m, jnp.max(sc, axis=1, keepdims=True))
        p = jnp.exp2(sc - m_new).astype(BF16)
        acc = acc * jnp.exp2(m - m_new) + _dot(p, v_ref[0, 0, pl.ds(k0, tk), :])
        return m_new, acc

    m0 = jnp.full((rows, 1), NEG_INF, F32)
    acc0 = jnp.zeros((rows, LANES), F32)
    _, acc = lax.fori_loop(0, s // tk, step, (m0, acc0), unroll=True)
    out = acc / acc[:, HEAD_DIM:HEAD_DIM + 1]
    for c in range(2):
        even = out[(2 * c) * tq:(2 * c + 1) * tq]
        odd = pltpu.roll(out[(2 * c + 1) * tq:(2 * c + 2) * tq], HEAD_DIM, 1)
        o_ref[0, :, c * LANES:(c + 1) * LANES] = jnp.where(first, even, odd).astype(BF16)


def _gqa_flash(qb, kt, vb, tq, tk):
    b, s, _ = qb.shape
    group_w = B_WIDTH // B_KV_HEADS
    return pl.pallas_call(
        functools.partial(_gqa_kernel, tk=tk),
        grid=(b, B_KV_HEADS, s // tq),
        in_specs=[pl.BlockSpec((1, tq, group_w), lambda i, j, t: (i, t, j)),
                  pl.BlockSpec((1, 1, LANES, s), lambda i, j, t: (i, j, 0, 0)),
                  pl.BlockSpec((1, 1, s, LANES), lambda i, j, t: (i, j, 0, 0))],
        out_specs=pl.BlockSpec((1, tq, group_w), lambda i, j, t: (i, t, j)),
        out_shape=jax.ShapeDtypeStruct((b, s, B_WIDTH), BF16),
        compiler_params=_params("parallel", "parallel", "arbitrary"),
        name="gqa_flash",
    )(qb, kt, vb)


A_CHUNK = 128
A_RADIUS = 64


def _dilated_kernel(q_ref, k_ref, v_ref, o_ref, stage, q0d, q1d, kd, vd, bias, ores, lres, onat,
                    lnat):
    s = q_ref.shape[1]
    n_chunks = s // A_CHUNK
    kw = 2 * A_CHUNK
    lane = lax.broadcasted_iota(jnp.int32, (1, LANES), 1)
    first = lane < HEAD_DIM

    tail = kw - A_RADIUS
    kd[0:A_RADIUS, :] = jnp.zeros((A_RADIUS, LANES), BF16)
    kd[A_RADIUS + s:A_RADIUS + s + tail, :] = jnp.zeros((tail, LANES), BF16)
    vd[0:A_RADIUS, 0:LANES] = jnp.zeros((A_RADIUS, LANES), BF16)
    vd[A_RADIUS + s:A_RADIUS + s + tail, 0:LANES] = jnp.zeros((tail, LANES), BF16)
    vd[:, LANES:2 * LANES] = jnp.ones((vd.shape[0], LANES), BF16)

    ri = lax.broadcasted_iota(jnp.int32, (A_CHUNK, kw), 0)
    ci = lax.broadcasted_iota(jnp.int32, (A_CHUNK, kw), 1)
    band = (ci >= ri) & (ci <= ri + 2 * A_RADIUS)
    for at_start in (0, 1):
        for at_end in (0, 1):
            ok = band
            if at_start:
                ok = ok & (ci >= A_RADIUS)
            if at_end:
                ok = ok & (ci < A_RADIUS + A_CHUNK)
            bias[at_start + 2 * at_end] = jnp.where(ok, 0.0, NEG_INF)

    for p, (window, dil) in enumerate(DILATED_PATTERNS):
        assert window // (2 * dil) == A_RADIUS
        cls_len = s // dil
        cls_chunks = cls_len // A_CHUNK

        def put(dst, r, val, pad):
            dst[pad + r * cls_len:pad + (r + 1) * cls_len, 0:LANES] = val

        if dil == 1:
            qv = q_ref[0]
            q0d[...] = jnp.where(first, qv, jnp.zeros((), BF16))
            q1d[...] = jnp.where(first, jnp.zeros((), BF16), qv)
            kd[A_RADIUS:A_RADIUS + s, :] = k_ref[0]
            vd[A_RADIUS:A_RADIUS + s, 0:LANES] = v_ref[0]
        else:
            stage[...] = q_ref[0].astype(F32)
            for r in range(dil):
                qv = stage[pl.ds(r, cls_len, stride=dil), :]
                put(q0d, r, jnp.where(first, qv, 0.0).astype(BF16), 0)
                put(q1d, r, jnp.where(first, 0.0, qv).astype(BF16), 0)
            stage[...] = k_ref[0].astype(F32)
            for r in range(dil):
                put(kd, r, stage[pl.ds(r, cls_len, stride=dil), :].astype(BF16), A_RADIUS)
            stage[...] = v_ref[0].astype(F32)
            for r in range(dil):
                put(vd, r, stage[pl.ds(r, cls_len, stride=dil), :].astype(BF16), A_RADIUS)

        o_dst, l_dst = (onat.at[p], lnat.at[p]) if dil == 1 else (ores, lres)

        def chunk(c, carry, cls_chunks=cls_chunks, o_dst=o_dst, l_dst=l_dst):
            r0 = pl.multiple_of(c * A_CHUNK, A_CHUNK)
            qq = jnp.concatenate([q0d[pl.ds(r0, A_CHUNK), :], q1d[pl.ds(r0, A_CHUNK), :]], axis=0)
            kwin = kd[pl.ds(r0, kw), :]
            vwin = vd[pl.ds(r0, kw), :]
            sc = lax.dot_general(qq, kwin, (((1,), (1,)), ((), ())), preferred_element_type=F32)
            in_cls = c % cls_chunks
            which = (in_cls == 0).astype(jnp.int32) + 2 * (in_cls == cls_chunks - 1).astype(jnp.int32)
            mask = bias[which]
            sc = sc + jnp.concatenate([mask, mask], axis=0)
            m = jnp.max(sc, axis=1, keepdims=True)
            pr = jnp.exp2(sc - m).astype(BF16)
            pv = _dot(pr, vwin)
            l = pv[:, LANES:]
            o2 = pv[:, :LANES] * (1.0 / l)
            lse2 = m + jnp.log2(l)
            o_dst[pl.ds(r0, A_CHUNK), :] = jnp.where(first, o2[:A_CHUNK], o2[A_CHUNK:])
            l_dst[pl.ds(r0, A_CHUNK), :] = jnp.where(first, lse2[:A_CHUNK], lse2[A_CHUNK:])
            return carry

        lax.fori_loop(0, n_chunks, chunk, 0, unroll=8)

        if dil != 1:
            for r in range(dil):
                onat[p, pl.ds(r, cls_len, stride=dil), :] = ores[r * cls_len:(r + 1) * cls_len, :]
                lnat[p, pl.ds(r, cls_len, stride=dil), :] = lres[r * cls_len:(r + 1) * cls_len, :]

    n_pat = len(DILATED_PATTERNS)
    blk = 256

    def mix(i, carry):
        r0 = pl.multiple_of(i * blk, blk)
        ls = [lnat[p, pl.ds(r0, blk), :] for p in range(n_pat)]
        mx = functools.reduce(jnp.maximum, ls)
        ws = [jnp.exp2(l - mx) for l in ls]
        num = sum(w * onat[p, pl.ds(r0, blk), :] for p, w in enumerate(ws))
        o_ref[0, pl.ds(r0, blk), :] = (num / sum(ws)).astype(BF16)
        return carry

    lax.fori_loop(0, s // blk, mix, 0)


def _dilated_attn(qa, ka, va):
    b, s, w = qa.shape
    n_pat = len(DILATED_PATTERNS)
    spec = pl.BlockSpec((1, s, LANES), lambda i, j: (i, 0, j))
    pad_rows = s + 2 * A_CHUNK
    return pl.pallas_call(
        _dilated_kernel,
        grid=(b, w // LANES),
        in_specs=[spec, spec, spec],
        out_specs=spec,
        out_shape=jax.ShapeDtypeStruct((b, s, w), BF16),
        scratch_shapes=[pltpu.VMEM((s, LANES), F32),
                        pltpu.VMEM((s, LANES), BF16),
                        pltpu.VMEM((s, LANES), BF16),
                        pltpu.VMEM((pad_rows, LANES), BF16),
                        pltpu.VMEM((pad_rows, 2 * LANES), BF16),
                        pltpu.VMEM((4, A_CHUNK, 2 * A_CHUNK), F32),
                        pltpu.VMEM((s, LANES), F32),
                        pltpu.VMEM((s, LANES), F32),
                        pltpu.VMEM((n_pat, s, LANES), F32),
                        pltpu.VMEM((n_pat, s, LANES), F32)],
        compiler_params=_params("parallel", "parallel"),
        name="dilated_attn",
    )(qa, ka, va)


def _attn_back_kernel(x_ref, oa_ref, ob_ref, g_ref, gate_ref, post_ref, w_ref, o_ref):
    ya = (oa_ref[0].astype(F32) * g_ref[0, :, :A_WIDTH].astype(F32)).astype(BF16)
    yb = (ob_ref[0].astype(F32) * g_ref[0, :, A_WIDTH:].astype(F32)).astype(BF16)
    m = _dot(ya, w_ref[:A_WIDTH, :]) + _dot(yb, w_ref[A_WIDTH:, :])
    o_ref[0] = _post_residual(x_ref[0], m, post_ref[...], gate_ref[0])


def _attn_back(x, oa, ob, g, gate, post_g, w_out, tm):
    b, s, d = x.shape
    row = lambda i, j: (i, j, 0)
    per_b = lambda i, j: (i, 0, 0)
    const = lambda i, j: (0, 0)
    return pl.pallas_call(
        _attn_back_kernel,
        grid=(b, s // tm),
        in_specs=[pl.BlockSpec((1, tm, d), row), pl.BlockSpec((1, tm, A_WIDTH), row),
                  pl.BlockSpec((1, tm, B_WIDTH), row), pl.BlockSpec((1, tm, MIX_WIDTH), row),
                  pl.BlockSpec((1, 1, d), per_b), pl.BlockSpec((1, d), const),
                  pl.BlockSpec((MIX_WIDTH, d), const)],
        out_specs=pl.BlockSpec((1, tm, d), row),
        out_shape=jax.ShapeDtypeStruct((b, s, d), F32),
        compiler_params=_params("parallel", "parallel"),
        name="attn_back",
    )(x, oa, ob, g, gate, post_g, w_out)


POOL_HALO = SUBLANES


def _pool_kernel(xp_ref, x_ref, xn_ref, shift_ref, scl_ref, gate_ref, pre_ref, post_ref,
                 w_in_ref, w_grp_ref, scale_ref, w_out_ref, o_ref, u_scr, *, seq):
    tm = x_ref.shape[1]
    width = scale_ref.shape[1]
    gdim = width // POOL_GROUPS
    i = pl.program_id(1)
    x = x_ref[0]
    xe = jnp.concatenate([xp_ref[0], x, xn_ref[0]], axis=0)
    he = _mod_norm(xe, pre_ref[...], scl_ref[0], shift_ref[0]).astype(BF16)
    t_ext = i * tm - POOL_HALO + lax.broadcasted_iota(jnp.int32, (tm + 2 * POOL_HALO, 1), 0)
    u_ext = _dot(he, w_in_ref[:, :width])
    u_scr[...] = jnp.where((t_ext >= 0) & (t_ext < seq), u_ext, 0.0)
    gate_act = _silu(_dot(he[POOL_HALO:POOL_HALO + tm], w_in_ref[:, width:]))
    t = i * tm + lax.broadcasted_iota(jnp.int32, (tm, 1), 0)
    m = jnp.zeros((tm, o_ref.shape[2]), F32)
    for g, window in enumerate(POOL_WINDOWS):
        half = window // 2
        assert half <= POOL_HALO
        cols = slice(g * gdim, (g + 1) * gdim)
        win = u_scr[POOL_HALO - half:POOL_HALO - half + tm, cols]
        for dd in range(-half + 1, half):
            win = win + u_scr[POOL_HALO + dd:POOL_HALO + dd + tm, cols]
        cnt = (jnp.minimum(t + half, seq) - jnp.maximum(t - half, 0)).astype(F32)
        pooled = win / cnt - u_scr[POOL_HALO:POOL_HALO + tm, cols]
        mixed = _dot(pooled.astype(BF16), w_grp_ref[g]) * scale_ref[:, cols]
        y = (mixed * gate_act[:, cols]).astype(BF16)
        m = m + _dot(y, w_out_ref[cols, :])
    o_ref[0] = _post_residual(x, m, post_ref[...], gate_ref[0])


def _pool_layer(x, shift, scl, gate, pre_g, post_g, w_in, w_grp, scale, w_out, tm):
    b, s, d = x.shape
    width = scale.shape[1]
    nt = tm // POOL_HALO
    last = s // POOL_HALO - 1
    row = lambda i, j: (i, j, 0)
    per_b = lambda i, j: (i, 0, 0)
    const = lambda i, j: (0, 0)
    return pl.pallas_call(
        functools.partial(_pool_kernel, seq=s),
        grid=(b, s // tm),
        in_specs=[pl.BlockSpec((1, POOL_HALO, d), lambda i, j: (i, jnp.maximum(j * nt - 1, 0), 0)),
                  pl.BlockSpec((1, tm, d), row),
                  pl.BlockSpec((1, POOL_HALO, d), lambda i, j: (i, jnp.minimum((j + 1) * nt, last), 0)),
                  pl.BlockSpec((1, 1, d), per_b), pl.BlockSpec((1, 1, d), per_b),
                  pl.BlockSpec((1, 1, d), per_b), pl.BlockSpec((1, d), const),
                  pl.BlockSpec((1, d), const), pl.BlockSpec((d, 2 * width), const),
                  pl.BlockSpec(w_grp.shape, lambda i, j: (0, 0, 0)),
                  pl.BlockSpec((1, width), const), pl.BlockSpec((width, d), const)],
        out_specs=pl.BlockSpec((1, tm, d), row),
        out_shape=jax.ShapeDtypeStruct((b, s, d), F32),
        scratch_shapes=[pltpu.VMEM((tm + 2 * POOL_HALO, width), F32)],
        compiler_params=_params("parallel", "parallel"),
        name="pool_layer",
    )(x, x, x, shift, scl, gate, pre_g, post_g, w_in, w_grp, scale, w_out)


def _trunk(x, mods, pre_norm, post_norm, attn_w_in, attn_q_norm, attn_k_norm, attn_w_out,
           pool_w_in, pool_w_grp, pool_scale, pool_w_out, tabs, bd, *, tm, tq, tk):
    depth = pre_norm.shape[0]
    d = x.shape[-1]
    for l in range(depth):
        i = l // 2
        shift, scl, gate = (mods[l][:, None, j * d:(j + 1) * d] for j in range(3))
        pre_g = pre_norm[l][None, :]
        post_g = post_norm[l][None, :]
        if l % 2 == 0:
            gq = jnp.tile(attn_q_norm[i], LANES // HEAD_DIM)[None, :]
            gk = jnp.tile(attn_k_norm[i], LANES // HEAD_DIM)[None, :]
            qa, ka, va, qb, kt, vb, g = _attn_front(x, shift, scl, pre_g, attn_w_in[i], gq, gk, bd,
                                                    tabs, tm)
            oa = _dilated_attn(qa, ka, va)
            ob = _gqa_flash(qb, kt, vb, tq, tk)
            x = _attn_back(x, oa, ob, g, gate, post_g, attn_w_out[i], tm)
        else:
            x = _pool_layer(x, shift, scl, gate, pre_g, post_g, pool_w_in[i], pool_w_grp[i],
                            pool_scale[i][None, :], pool_w_out[i], tm)
    return x


def kernel(x_prompt, x_sample, c_prompt, c_sample, ada_w, ada_b, pre_norm, post_norm, attn_w_in, attn_q_norm, attn_k_norm, attn_w_out, pool_w_in, pool_w_grp, pool_scale, pool_w_out):
    nb_p = x_prompt.shape[0]
    mods = _ada_mod(jnp.concatenate([c_prompt, c_sample], axis=0), ada_w, ada_b)
    head_of_lane = jnp.arange(LANES) // HEAD_DIM
    bd = (head_of_lane[:, None] == head_of_lane[None, :]).astype(BF16)
    weights = (pre_norm, post_norm, attn_w_in.astype(BF16), attn_q_norm, attn_k_norm,
               attn_w_out.astype(BF16), pool_w_in.astype(BF16), pool_w_grp.astype(BF16), pool_scale,
               pool_w_out.astype(BF16))
    outs = []
    for x, sl in ((x_prompt, slice(0, nb_p)), (x_sample, slice(nb_p, None))):
        tabs = _make_tables(x.shape[1])
        outs.append(_trunk(x, mods[:, sl], *weights, tabs, bd, tm=512, tq=256, tk=512))
    return tuple(outs)
```

```python
import functools
import math

import jax
import jax.numpy as jnp
from jax import lax
from jax.experimental import pallas as pl
from jax.experimental.pallas import tpu as pltpu

HEAD_DIM = 64
A_HEADS = 8
B_HEADS = 8
B_KV_HEADS = 2
A_WIDTH = A_HEADS * HEAD_DIM
B_WIDTH = B_HEADS * HEAD_DIM
B_KV_WIDTH = B_KV_HEADS * HEAD_DIM
MIX_WIDTH = A_WIDTH + B_WIDTH
DILATED_PATTERNS = ((128, 1), (512, 4), (2048, 16))
ROPE_THETA = 500000.0
ROPE_DIM = HEAD_DIM // 4
AXIAL_THETA = 10000.0
GRID_W = 64
POOL_WINDOWS = (2, 4, 8, 16)
POOL_GROUPS = 4
NORM_EPS = 1e-6
NEG_INF = -1e30

LANES = 128
SUBLANES = 8
MXU_DIM = 256
VMEM_LIMIT_BYTES = 56 * 1024 * 1024
Q_SCALE = math.log2(math.e) / math.sqrt(HEAD_DIM)

BF16 = jnp.bfloat16
F32 = jnp.float32


def _params(*semantics):
    return pltpu.CompilerParams(dimension_semantics=semantics, vmem_limit_bytes=VMEM_LIMIT_BYTES)


def _dot(a, b):
    return jnp.dot(a, b, preferred_element_type=F32)


def _silu(x):
    return x * (1.0 / (1.0 + jnp.exp(-x)))


def _mod_norm(x, pre_g, scl, shift):
    ms = jnp.mean(x * x, axis=-1, keepdims=True)
    return (x * lax.rsqrt(ms + NORM_EPS) * pre_g) * (1.0 + scl) + shift


def _post_residual(x, m, post_g, gate):
    ms = jnp.mean(m * m, axis=-1, keepdims=True)
    return x + gate * (m * lax.rsqrt(ms + NORM_EPS) * post_g)


def _ada_kernel(c_ref, w_ref, b_ref, o_ref):
    a = _silu(c_ref[...]).astype(BF16)
    o_ref[0] = _dot(a, w_ref[0].astype(BF16)) + b_ref[0]


def _ada_mod(c_all, ada_w, ada_b):
    depth, d, d3 = ada_w.shape
    nb = c_all.shape[0]
    tn = d
    return pl.pallas_call(
        _ada_kernel,
        grid=(depth, d3 // tn),
        in_specs=[pl.BlockSpec((nb, d), lambda l, j: (0, 0)),
                  pl.BlockSpec((1, d, tn), lambda l, j: (l, 0, j)),
                  pl.BlockSpec((1, 1, tn), lambda l, j: (l, 0, j))],
        out_specs=pl.BlockSpec((1, nb, tn), lambda l, j: (l, 0, j)),
        out_shape=jax.ShapeDtypeStruct((depth, nb, d3), F32),
        compiler_params=_params("arbitrary", "arbitrary"),
        name="ada_mod",
    )(c_all, ada_w, ada_b.reshape(depth, 1, d3))


def _rope_tables(pos, rot_dim, theta, lane_in_block):
    h = rot_dim // 2
    f = jnp.where(lane_in_block >= 0, lane_in_block % h, 0)
    inv = theta ** (-(2.0 * f.astype(F32)) / rot_dim)
    ang = pos.astype(F32)[:, None] * inv[None, :]
    active = (lane_in_block >= 0)[None, :]
    lo = (lane_in_block < h)[None, :] & active
    hi = (lane_in_block >= h)[None, :] & active
    cos = jnp.where(active, jnp.cos(ang), 1.0)
    sin = jnp.sin(ang)
    return cos, jnp.where(lo, -sin, 0.0), jnp.where(hi, sin, 0.0)


def _make_tables(s):
    lane = jnp.arange(LANES) % HEAD_DIM
    pos = jnp.arange(s)
    a_tabs = _rope_tables(pos, ROPE_DIM, ROPE_THETA, jnp.where(lane < ROPE_DIM, lane, -1))
    half = HEAD_DIM // 2
    in_blk = lane % half
    row_t = _rope_tables(pos // GRID_W, half, AXIAL_THETA, in_blk)
    col_t = _rope_tables(pos % GRID_W, half, AXIAL_THETA, in_blk)
    first = (lane < half)[None, :]
    b_tabs = tuple(jnp.where(first, r, c) for r, c in zip(row_t, col_t))
    return a_tabs + b_tabs


def _apply_rope(x, cos, s_lo, s_hi, h):
    return x * cos + pltpu.roll(x, LANES - h, 1) * s_lo + pltpu.roll(x, h, 1) * s_hi


def _attn_front_kernel(x_ref, shift_ref, scl_ref, pre_ref, w_ref, gq_ref, gk_ref, bd_ref,
                       ac_ref, alo_ref, ahi_ref, bc_ref, blo_ref, bhi_ref,
                       qa_ref, ka_ref, va_ref, qb_ref, kt_ref, vb_ref, g_ref):
    h = _mod_norm(x_ref[0], pre_ref[...], scl_ref[0], shift_ref[0]).astype(BF16)
    lane = lax.broadcasted_iota(jnp.int32, (1, LANES), 1)
    a_tabs = (ac_ref[...], alo_ref[...], ahi_ref[...])
    b_tabs = (bc_ref[...], blo_ref[...], bhi_ref[...])
    def head_rms(xc, g):
        n = xc.shape[1]
        bd = bd_ref[0:n, 0:n]
        sq = xc * xc
        hi = sq.astype(BF16)
        lo = (sq - hi.astype(F32)).astype(BF16)
        ss = _dot(hi, bd) + _dot(lo, bd)
        g = jnp.concatenate([g] * (n // LANES), axis=1)
        return xc * lax.rsqrt(ss * (1.0 / HEAD_DIM) + NORM_EPS) * g

    off = 0
    qa = _dot(h, w_ref[:, off:off + A_WIDTH]); off += A_WIDTH
    for c in range(A_WIDTH // LANES):
        col = _apply_rope(qa[:, c * LANES:(c + 1) * LANES], *a_tabs, ROPE_DIM // 2)
        qa_ref[0, :, c * LANES:(c + 1) * LANES] = (col * Q_SCALE).astype(BF16)
    ka = _dot(h, w_ref[:, off:off + A_WIDTH]); off += A_WIDTH
    for c in range(A_WIDTH // LANES):
        col = _apply_rope(ka[:, c * LANES:(c + 1) * LANES], *a_tabs, ROPE_DIM // 2)
        ka_ref[0, :, c * LANES:(c + 1) * LANES] = col.astype(BF16)
    va_ref[0] = _dot(h, w_ref[:, off:off + A_WIDTH]).astype(BF16); off += A_WIDTH

    qb = _dot(h, w_ref[:, off:off + B_WIDTH]); off += B_WIDTH
    wide = bd_ref.shape[0]
    for c2 in range(B_WIDTH // wide):
        blk = head_rms(qb[:, c2 * wide:(c2 + 1) * wide], gq_ref[...])
        for c in range(wide // LANES):
            col = _apply_rope(blk[:, c * LANES:(c + 1) * LANES], *b_tabs, HEAD_DIM // 4)
            dst = c2 * wide + c * LANES
            qb_ref[0, :, dst:dst + LANES] = (col * Q_SCALE).astype(BF16)

    kb = _dot(h, w_ref[:, off:off + B_KV_WIDTH]); off += B_KV_WIDTH
    kb = _apply_rope(head_rms(kb, gk_ref[...]), *b_tabs, HEAD_DIM // 4)
    kt = kb.T
    for j in range(B_KV_HEADS):
        kj = kt[j * HEAD_DIM:(j + 1) * HEAD_DIM].astype(BF16)
        kt_ref[0, j] = jnp.concatenate([kj, kj], axis=0)

    vb = _dot(h, w_ref[:, off:off + B_KV_WIDTH]); off += B_KV_WIDTH
    ones_col = jnp.where(lane == HEAD_DIM, 1.0, 0.0)
    for j in range(B_KV_HEADS):
        vj = vb if j == 0 else pltpu.roll(vb, HEAD_DIM, 1)
        vb_ref[0, j] = jnp.where(lane < HEAD_DIM, vj, ones_col).astype(BF16)

    g_ref[0] = _silu(_dot(h, w_ref[:, off:off + MIX_WIDTH])).astype(BF16)


def _attn_front(x, shift, scl, pre_g, w_in, gq, gk, bd, tabs, tm):
    b, s, d = x.shape
    n_in = w_in.shape[1]
    row = lambda i, j: (i, j, 0)
    per_b = lambda i, j: (i, 0, 0)
    const = lambda i, j: (0, 0)
    tab_spec = pl.BlockSpec((tm, LANES), lambda i, j: (j, 0))
    wide = lambda w: pl.BlockSpec((1, tm, w), row)
    return pl.pallas_call(
        _attn_front_kernel,
        grid=(b, s // tm),
        in_specs=[wide(d), pl.BlockSpec((1, 1, d), per_b), pl.BlockSpec((1, 1, d), per_b),
                  pl.BlockSpec((1, d), const), pl.BlockSpec((d, n_in), const),
                  pl.BlockSpec((1, LANES), const), pl.BlockSpec((1, LANES), const),
                  pl.BlockSpec(bd.shape, const)] + [tab_spec] * 6,
        out_specs=[wide(A_WIDTH), wide(A_WIDTH), wide(A_WIDTH), wide(B_WIDTH),
                   pl.BlockSpec((1, B_KV_HEADS, LANES, tm), lambda i, j: (i, 0, 0, j)),
                   pl.BlockSpec((1, B_KV_HEADS, tm, LANES), lambda i, j: (i, 0, j, 0)),
                   wide(MIX_WIDTH)],
        out_shape=[jax.ShapeDtypeStruct((b, s, A_WIDTH), BF16)] * 3
        + [jax.ShapeDtypeStruct((b, s, B_WIDTH), BF16),
           jax.ShapeDtypeStruct((b, B_KV_HEADS, LANES, s), BF16),
           jax.ShapeDtypeStruct((b, B_KV_HEADS, s, LANES), BF16),
           jax.ShapeDtypeStruct((b, s, MIX_WIDTH), BF16)],
        compiler_params=_params("parallel", "parallel"),
        name="attn_front",
    )(x, shift, scl, pre_g, w_in, gq, gk, bd, *tabs)


def _gqa_kernel(q_ref, kt_ref, v_ref, o_ref, *, tk):
    tq = q_ref.shape[1]
    s = kt_ref.shape[3]
    lane = lax.broadcasted_iota(jnp.int32, (1, LANES), 1)
    first = lane < HEAD_DIM
    zero = jnp.zeros((), BF16)
    parts = []
    for c in range(2):
        qc = q_ref[0, :, c * LANES:(c + 1) * LANES]
        parts += [jnp.where(first, qc, zero), jnp.where(first, zero, qc)]
    qs = jnp.concatenate(parts, axis=0)
    rows = qs.shape[0]

    def step(i, carry):
        m, acc = carry
        k0 = pl.multiple_of(i * tk, tk)
        sc = _dot(qs, kt_ref[0, 0, :, pl.ds(k0, tk)])
        m_new = jnp.maximum(m, jnp.max(sc, axis=1, keepdims=True))
        p = jnp.exp2(sc - m_new).astype(BF16)
        acc = acc * jnp.exp2(m - m_new) + _dot(p, v_ref[0, 0, pl.ds(k0, tk), :])
        return m_new, acc

    m0 = jnp.full((rows, 1), NEG_INF, F32)
    acc0 = jnp.zeros((rows, LANES), F32)
    _, acc = lax.fori_loop(0, s // tk, step, (m0, acc0), unroll=True)
    out = acc / acc[:, HEAD_DIM:HEAD_DIM + 1]
    for c in range(2):
        even = out[(2 * c) * tq:(2 * c + 1) * tq]
        odd = pltpu.roll(out[(2 * c + 1) * tq:(2 * c + 2) * tq], HEAD_DIM, 1)
        o_ref[0, :, c * LANES:(c + 1) * LANES] = jnp.where(first, even, odd).astype(BF16)


def _gqa_flash(qb, kt, vb, tq, tk):
    b, s, _ = qb.shape
    group_w = B_WIDTH // B_KV_HEADS
    return pl.pallas_call(
        functools.partial(_gqa_kernel, tk=tk),
        grid=(b, B_KV_HEADS, s // tq),
        in_specs=[pl.BlockSpec((1, tq, group_w), lambda i, j, t: (i, t, j)),
                  pl.BlockSpec((1, 1, LANES, s), lambda i, j, t: (i, j, 0, 0)),
                  pl.BlockSpec((1, 1, s, LANES), lambda i, j, t: (i, j, 0, 0))],
        out_specs=pl.BlockSpec((1, tq, group_w), lambda i, j, t: (i, t, j)),
        out_shape=jax.ShapeDtypeStruct((b, s, B_WIDTH), BF16),
        compiler_params=_params("parallel", "parallel", "arbitrary"),
        name="gqa_flash",
    )(qb, kt, vb)


A_CHUNK = 128
A_RADIUS = 64
A_GROUP = 32
A_PERM_BLOCK = MXU_DIM
_A_SORTED_DILATIONS = tuple(d for _, d in DILATED_PATTERNS if d != 1)


def _class_perm(dil):
    i = jnp.arange(A_PERM_BLOCK)
    m = A_PERM_BLOCK // dil
    src = (i % m) * dil + i // m
    return (src[:, None] == i[None, :]).astype(BF16)


def _dilated_kernel(start_ref, q_ref, k_ref, v_ref, p4_ref, p16_ref, o_ref, q0d, q1d, kd, vd, bias,
                    st_a, st_b, tile):
    perm_refs = dict(zip(_A_SORTED_DILATIONS, (p4_ref, p16_ref)))
    s = q_ref.shape[1]
    n_chunks = s // A_CHUNK
    kw = 2 * A_CHUNK
    lane = lax.broadcasted_iota(jnp.int32, (1, LANES), 1)
    first = lane < HEAD_DIM

    tail = kw - A_RADIUS
    kd[0:A_RADIUS, :] = jnp.zeros((A_RADIUS, LANES), BF16)
    kd[A_RADIUS + s:A_RADIUS + s + tail, :] = jnp.zeros((tail, LANES), BF16)
    vd[0:A_RADIUS, 0:LANES] = jnp.zeros((A_RADIUS, LANES), BF16)
    vd[A_RADIUS + s:A_RADIUS + s + tail, 0:LANES] = jnp.zeros((tail, LANES), BF16)
    vd[:, LANES:2 * LANES] = jnp.ones((vd.shape[0], LANES), BF16)

    ri = lax.broadcasted_iota(jnp.int32, (A_CHUNK, kw), 0)
    ci = lax.broadcasted_iota(jnp.int32, (A_CHUNK, kw), 1)
    band = (ci >= ri) & (ci <= ri + 2 * A_RADIUS)
    for at_start in (0, 1):
        for at_end in (0, 1):
            ok = band
            if at_start:
                ok = ok & (ci >= A_RADIUS)
            if at_end:
                ok = ok & (ci < A_RADIUS + A_CHUNK)
            bias[at_start + 2 * at_end] = jnp.where(ok, 0.0, NEG_INF)

    order = sorted(DILATED_PATTERNS, key=lambda wd: -wd[1])
    states = (st_a, st_b)
    parent_dil = None
    for p, (window, dil) in enumerate(order):
        assert window // (2 * dil) == A_RADIUS
        cls_len = s // dil
        cls_chunks = cls_len // A_CHUNK
        final = p == len(order) - 1
        assert (dil == 1) == final
        src_state, dst_state = states[(p + 1) % 2], states[p % 2]

        if dil == 1:
            qv = q_ref[0]
            q0d[...] = jnp.where(first, qv, jnp.zeros((), BF16))
            q1d[...] = jnp.where(first, jnp.zeros((), BF16), qv)
            kd[A_RADIUS:A_RADIUS + s, :] = k_ref[0]
            vd[A_RADIUS:A_RADIUS + s, 0:LANES] = v_ref[0]
        else:
            perm = perm_refs[dil][...]
            piece = A_PERM_BLOCK // dil

            def sort_block(bk, carry, dil=dil, cls_len=cls_len, perm=perm, piece=piece):
                r0 = pl.multiple_of(bk * A_PERM_BLOCK, A_PERM_BLOCK)
                qv = q_ref[0, pl.ds(r0, A_PERM_BLOCK), :]
                zero = jnp.zeros((), BF16)
                qm = jnp.concatenate([jnp.where(first, qv, zero), jnp.where(first, zero, qv)], axis=1)
                kv = jnp.concatenate([k_ref[0, pl.ds(r0, A_PERM_BLOCK), :],
                                      v_ref[0, pl.ds(r0, A_PERM_BLOCK), :]], axis=1)
                yq = _dot(perm, qm).astype(BF16)
                ykv = _dot(perm, kv).astype(BF16)
                for r in range(dil):
                    src = slice(r * piece, (r + 1) * piece)
                    row = pl.multiple_of(r * cls_len + bk * piece, piece)
                    q0d[pl.ds(row, piece), :] = yq[src, :LANES]
                    q1d[pl.ds(row, piece), :] = yq[src, LANES:]
                    kd[pl.ds(A_RADIUS + row, piece), :] = ykv[src, :LANES]
                    vd[pl.ds(A_RADIUS + row, piece), 0:LANES] = ykv[src, LANES:]
                return carry

            lax.fori_loop(0, s // A_PERM_BLOCK, sort_block, 0, unroll=True)

        def chunk(c, slot, dil=dil, cls_chunks=cls_chunks, final=final, parent_dil=parent_dil,
                  src_state=src_state, dst_state=dst_state):
            r0 = pl.multiple_of(c * A_CHUNK, A_CHUNK)
            qq = jnp.concatenate([q0d[pl.ds(r0, A_CHUNK), :], q1d[pl.ds(r0, A_CHUNK), :]], axis=0)
            kwin = kd[pl.ds(r0, kw), :]
            vwin = vd[pl.ds(r0, kw), :]
            sc = lax.dot_general(qq, kwin, (((1,), (1,)), ((), ())), preferred_element_type=F32)
            in_cls = c % cls_chunks
            mask = bias[jnp.where(in_cls == 0, 1, 0) + jnp.where(in_cls == cls_chunks - 1, 2, 0)]
            sc = sc + jnp.concatenate([mask, mask], axis=0)
            m = jnp.max(sc, axis=1, keepdims=True)
            pr = jnp.exp2(sc - m).astype(BF16)
            pv = _dot(pr, vwin)
            pick = lambda x: jnp.where(first, x[:A_CHUNK], x[A_CHUNK:])
            num, den = pick(pv[:, :LANES]), pick(pv[:, LANES:])
            top = pick(jnp.broadcast_to(m, (2 * A_CHUNK, LANES)))
            if parent_dil is not None:
                fan = parent_dil // dil
                piece = A_CHUNK // fan
                cls = c // cls_chunks
                for a in range(fan):
                    row = pl.multiple_of((cls + dil * a) * (s // parent_dil) + in_cls * piece, piece)
                    for k in range(3):
                        tile[slot, k, pl.ds(a, piece, stride=fan), :] = src_state[k, pl.ds(row, piece), :]
                top_p = tile[slot, 1]
                top_new = jnp.maximum(top, top_p)
                w_c, w_p = jnp.exp2(top - top_new), jnp.exp2(top_p - top_new)
                num = w_c * num + w_p * tile[slot, 0]
                den = w_c * den + w_p * tile[slot, 2]
                top = top_new
            if final:
                o_ref[0, pl.ds(r0, A_CHUNK), :] = (num * (1.0 / den)).astype(BF16)
            else:
                dst_state[0, pl.ds(r0, A_CHUNK), :] = num
                dst_state[1, pl.ds(r0, A_CHUNK), :] = top
                dst_state[2, pl.ds(r0, A_CHUNK), :] = den

        group = min(A_GROUP, n_chunks)
        assert n_chunks % group == 0

        def chunk_group(g, carry, chunk=chunk, group=group):
            for slot in range(group):
                chunk(g * group + slot, slot)
            return carry

        lax.fori_loop(start_ref[0], start_ref[0] + n_chunks // group, chunk_group, 0)
        parent_dil = dil


def _dilated_attn(qa, ka, va):
    b, s, w = qa.shape
    spec = pl.BlockSpec((1, s, LANES), lambda i, j: (i, 0, j))
    pad_rows = s + 2 * A_CHUNK
    perm_spec = pl.BlockSpec((A_PERM_BLOCK, A_PERM_BLOCK), lambda i, j: (0, 0))
    perms = [_class_perm(d) for d in _A_SORTED_DILATIONS]
    return pl.pallas_call(
        _dilated_kernel,
        grid=(b, w // LANES),
        in_specs=[pl.BlockSpec(memory_space=pltpu.SMEM), spec, spec, spec] + [perm_spec] * len(perms),
        out_specs=spec,
        out_shape=jax.ShapeDtypeStruct((b, s, w), BF16),
        scratch_shapes=[pltpu.VMEM((s, LANES), BF16),
                        pltpu.VMEM((s, LANES), BF16),
                        pltpu.VMEM((pad_rows, LANES), BF16),
                        pltpu.VMEM((pad_rows, 2 * LANES), BF16),
                        pltpu.VMEM((4, A_CHUNK, 2 * A_CHUNK), F32),
                        pltpu.VMEM((3, s, LANES), F32),
                        pltpu.VMEM((3, s, LANES), F32),
                        pltpu.VMEM((A_GROUP, 3, A_CHUNK, LANES), F32)],
        compiler_params=_params("parallel", "parallel"),
        name="dilated_attn",
    )(jnp.zeros((1,), jnp.int32), qa, ka, va, *perms)


def _attn_back_kernel(x_ref, oa_ref, ob_ref, g_ref, gate_ref, post_ref, w_ref, o_ref):
    ya = (oa_ref[0].astype(F32) * g_ref[0, :, :A_WIDTH].astype(F32)).astype(BF16)
    yb = (ob_ref[0].astype(F32) * g_ref[0, :, A_WIDTH:].astype(F32)).astype(BF16)
    m = _dot(ya, w_ref[:A_WIDTH, :]) + _dot(yb, w_ref[A_WIDTH:, :])
    o_ref[0] = _post_residual(x_ref[0], m, post_ref[...], gate_ref[0])


def _attn_back(x, oa, ob, g, gate, post_g, w_out, tm):
    b, s, d = x.shape
    row = lambda i, j: (i, j, 0)
    per_b = lambda i, j: (i, 0, 0)
    const = lambda i, j: (0, 0)
    return pl.pallas_call(
        _attn_back_kernel,
        grid=(b, s // tm),
        in_specs=[pl.BlockSpec((1, tm, d), row), pl.BlockSpec((1, tm, A_WIDTH), row),
                  pl.BlockSpec((1, tm, B_WIDTH), row), pl.BlockSpec((1, tm, MIX_WIDTH), row),
                  pl.BlockSpec((1, 1, d), per_b), pl.BlockSpec((1, d), const),
                  pl.BlockSpec((MIX_WIDTH, d), const)],
        out_specs=pl.BlockSpec((1, tm, d), row),
        out_shape=jax.ShapeDtypeStruct((b, s, d), F32),
        compiler_params=_params("parallel", "parallel"),
        name="attn_back",
    )(x, oa, ob, g, gate, post_g, w_out)


POOL_HALO = SUBLANES
POOL_TAIL = 2 * SUBLANES


def _pool_kernel(xp_ref, x_ref, xn_ref, shift_ref, scl_ref, gate_ref, pre_ref, post_ref,
                 w_in_ref, w_grp_ref, scale_ref, w_out_ref, o_ref, u_scr, c_a, c_b, *, seq):
    tm = x_ref.shape[1]
    width = scale_ref.shape[1]
    gdim = width // POOL_GROUPS
    ext = tm + 2 * POOL_HALO
    i = pl.program_id(1)
    x = x_ref[0]
    xe = jnp.concatenate([xp_ref[0], x, xn_ref[0]], axis=0)
    he = _mod_norm(xe, pre_ref[...], scl_ref[0], shift_ref[0]).astype(BF16)
    t_ext = i * tm - POOL_HALO + lax.broadcasted_iota(jnp.int32, (ext, 1), 0)
    u_ext = _dot(he, w_in_ref[:, :width])
    u_scr[0:ext, :] = jnp.where((t_ext >= 0) & (t_ext < seq), u_ext, 0.0)
    u_scr[ext:ext + POOL_TAIL, :] = jnp.zeros((POOL_TAIL, width), F32)
    gate_act = _silu(_dot(he[POOL_HALO:POOL_HALO + tm], w_in_ref[:, width:]))
    t = i * tm + lax.broadcasted_iota(jnp.int32, (tm, 1), 0)
    m = jnp.zeros((tm, o_ref.shape[2]), F32)
    for g, window in enumerate(POOL_WINDOWS):
        half = window // 2
        assert half <= POOL_HALO
        cols = slice(g * gdim, (g + 1) * gdim)
        read = lambda e, n, cols=cols: u_scr[e:e + n, cols]
        span, n, bufs = 1, ext + POOL_TAIL, (c_a, c_b)
        while span < half:
            n -= SUBLANES
            dst = bufs[0]
            dst[0:n, :] = read(0, n) + read(span, n)
            read = lambda e, n, dst=dst: dst[e:e + n, :]
            span, bufs = 2 * span, bufs[::-1]
        win = read(POOL_HALO - half, tm) + read(POOL_HALO, tm)
        cnt = (jnp.minimum(t + half, seq) - jnp.maximum(t - half, 0)).astype(F32)
        pooled = win / cnt - u_scr[POOL_HALO:POOL_HALO + tm, cols]
        mixed = _dot(pooled.astype(BF16), w_grp_ref[g]) * scale_ref[:, cols]
        y = (mixed * gate_act[:, cols]).astype(BF16)
        m = m + _dot(y, w_out_ref[cols, :])
    o_ref[0] = _post_residual(x, m, post_ref[...], gate_ref[0])


def _pool_layer(x, shift, scl, gate, pre_g, post_g, w_in, w_grp, scale, w_out, tm):
    b, s, d = x.shape
    width = scale.shape[1]
    nt = tm // POOL_HALO
    last = s // POOL_HALO - 1
    row = lambda i, j: (i, j, 0)
    per_b = lambda i, j: (i, 0, 0)
    const = lambda i, j: (0, 0)
    return pl.pallas_call(
        functools.partial(_pool_kernel, seq=s),
        grid=(b, s // tm),
        in_specs=[pl.BlockSpec((1, POOL_HALO, d), lambda i, j: (i, jnp.maximum(j * nt - 1, 0), 0)),
                  pl.BlockSpec((1, tm, d), row),
                  pl.BlockSpec((1, POOL_HALO, d), lambda i, j: (i, jnp.minimum((j + 1) * nt, last), 0)),
                  pl.BlockSpec((1, 1, d), per_b), pl.BlockSpec((1, 1, d), per_b),
                  pl.BlockSpec((1, 1, d), per_b), pl.BlockSpec((1, d), const),
                  pl.BlockSpec((1, d), const), pl.BlockSpec((d, 2 * width), const),
                  pl.BlockSpec(w_grp.shape, lambda i, j: (0, 0, 0)),
                  pl.BlockSpec((1, width), const), pl.BlockSpec((width, d), const)],
        out_specs=pl.BlockSpec((1, tm, d), row),
        out_shape=jax.ShapeDtypeStruct((b, s, d), F32),
        scratch_shapes=[pltpu.VMEM((tm + 2 * POOL_HALO + POOL_TAIL, width), F32),
                        pltpu.VMEM((tm + 2 * POOL_HALO + POOL_TAIL, width // POOL_GROUPS), F32),
                        pltpu.VMEM((tm + 2 * POOL_HALO + POOL_TAIL, width // POOL_GROUPS), F32)],
        compiler_params=_params("parallel", "parallel"),
        name="pool_layer",
    )(x, x, x, shift, scl, gate, pre_g, post_g, w_in, w_grp, scale, w_out)


def _trunk(x, mods, pre_norm, post_norm, attn_w_in, attn_q_norm, attn_k_norm, attn_w_out,
           pool_w_in, pool_w_grp, pool_scale, pool_w_out, tabs, bd, *, tm, tq, tk):
    depth = pre_norm.shape[0]
    d = x.shape[-1]
    for l in range(depth):
        i = l // 2
        shift, scl, gate = (mods[l][:, None, j * d:(j + 1) * d] for j in range(3))
        pre_g = pre_norm[l][None, :]
        post_g = post_norm[l][None, :]
        if l % 2 == 0:
            gq = jnp.tile(attn_q_norm[i], LANES // HEAD_DIM)[None, :]
            gk = jnp.tile(attn_k_norm[i], LANES // HEAD_DIM)[None, :]
            qa, ka, va, qb, kt, vb, g = _attn_front(x, shift, scl, pre_g, attn_w_in[i], gq, gk, bd,
                                                    tabs, tm)
            oa = _dilated_attn(qa, ka, va)
            ob = _gqa_flash(qb, kt, vb, tq, tk)
            x = _attn_back(x, oa, ob, g, gate, post_g, attn_w_out[i], tm)
        else:
            x = _pool_layer(x, shift, scl, gate, pre_g, post_g, pool_w_in[i], pool_w_grp[i],
                            pool_scale[i][None, :], pool_w_out[i], tm)
    return x


def kernel(x_prompt, x_sample, c_prompt, c_sample, ada_w, ada_b, pre_norm, post_norm, attn_w_in, attn_q_norm, attn_k_norm, attn_w_out, pool_w_in, pool_w_grp, pool_scale, pool_w_out):
    nb_p = x_prompt.shape[0]
    mods = _ada_mod(jnp.concatenate([c_prompt, c_sample], axis=0), ada_w, ada_b)
    head_of_lane = jnp.arange(MXU_DIM) // HEAD_DIM
    bd = (head_of_lane[:, None] == head_of_lane[None, :]).astype(BF16)
    weights = (pre_norm, post_norm, attn_w_in.astype(BF16), attn_q_norm, attn_k_norm,
               attn_w_out.astype(BF16), pool_w_in.astype(BF16), pool_w_grp.astype(BF16), pool_scale,
               pool_w_out.astype(BF16))
    outs = []
    for x, sl in ((x_prompt, slice(0, nb_p)), (x_sample, slice(nb_p, None))):
        tabs = _make_tables(x.shape[1])
        outs.append(_trunk(x, mods[:, sl], *weights, tabs, bd, tm=512, tq=256, tk=512))
    return tuple(outs)
```

```python
import functools
import math

import jax
import jax.numpy as jnp
from jax import lax
from jax.experimental import pallas as pl
from jax.experimental.pallas import tpu as pltpu

HEAD_DIM = 64
A_HEADS = 8
B_HEADS = 8
B_KV_HEADS = 2
A_WIDTH = A_HEADS * HEAD_DIM
B_WIDTH = B_HEADS * HEAD_DIM
B_KV_WIDTH = B_KV_HEADS * HEAD_DIM
MIX_WIDTH = A_WIDTH + B_WIDTH
DILATED_PATTERNS = ((128, 1), (512, 4), (2048, 16))
ROPE_THETA = 500000.0
ROPE_DIM = HEAD_DIM // 4
AXIAL_THETA = 10000.0
GRID_W = 64
POOL_WINDOWS = (2, 4, 8, 16)
POOL_GROUPS = 4
NORM_EPS = 1e-6
NEG_INF = -1e30

LANES = 128
SUBLANES = 8
MXU_DIM = 256
VMEM_LIMIT_BYTES = 56 * 1024 * 1024
Q_SCALE = math.log2(math.e) / math.sqrt(HEAD_DIM)

BF16 = jnp.bfloat16
F32 = jnp.float32


def _params(*semantics):
    return pltpu.CompilerParams(dimension_semantics=semantics, vmem_limit_bytes=VMEM_LIMIT_BYTES)


def _dot(a, b):
    return jnp.dot(a, b, preferred_element_type=F32)


def _silu(x):
    return x * (1.0 / (1.0 + jnp.exp(-x)))


def _mod_norm(x, pre_g, scl, shift):
    ms = jnp.mean(x * x, axis=-1, keepdims=True)
    return (x * lax.rsqrt(ms + NORM_EPS) * pre_g) * (1.0 + scl) + shift


def _post_residual(x, m, post_g, gate):
    ms = jnp.mean(m * m, axis=-1, keepdims=True)
    return x + gate * (m * lax.rsqrt(ms + NORM_EPS) * post_g)


def _ada_kernel(c_ref, w_ref, b_ref, o_ref):
    a = _silu(c_ref[...]).astype(BF16)
    o_ref[0] = _dot(a, w_ref[0].astype(BF16)) + b_ref[0]


def _ada_mod(c_all, ada_w, ada_b):
    depth, d, d3 = ada_w.shape
    nb = c_all.shape[0]
    tn = d
    return pl.pallas_call(
        _ada_kernel,
        grid=(depth, d3 // tn),
        in_specs=[pl.BlockSpec((nb, d), lambda l, j: (0, 0)),
                  pl.BlockSpec((1, d, tn), lambda l, j: (l, 0, j)),
                  pl.BlockSpec((1, 1, tn), lambda l, j: (l, 0, j))],
        out_specs=pl.BlockSpec((1, nb, tn), lambda l, j: (l, 0, j)),
        out_shape=jax.ShapeDtypeStruct((depth, nb, d3), F32),
        compiler_params=_params("arbitrary", "arbitrary"),
        name="ada_mod",
    )(c_all, ada_w, ada_b.reshape(depth, 1, d3))


def _rope_tables(pos, rot_dim, theta, lane_in_block):
    h = rot_dim // 2
    f = jnp.where(lane_in_block >= 0, lane_in_block % h, 0)
    inv = theta ** (-(2.0 * f.astype(F32)) / rot_dim)
    ang = pos.astype(F32)[:, None] * inv[None, :]
    active = (lane_in_block >= 0)[None, :]
    lo = (lane_in_block < h)[None, :] & active
    hi = (lane_in_block >= h)[None, :] & active
    cos = jnp.where(active, jnp.cos(ang), 1.0)
    sin = jnp.sin(ang)
    return cos, jnp.where(lo, -sin, 0.0), jnp.where(hi, sin, 0.0)


def _make_tables(s):
    lane = jnp.arange(LANES) % HEAD_DIM
    pos = jnp.arange(s)
    a_tabs = _rope_tables(pos, ROPE_DIM, ROPE_THETA, jnp.where(lane < ROPE_DIM, lane, -1))
    half = HEAD_DIM // 2
    in_blk = lane % half
    row_t = _rope_tables(pos // GRID_W, half, AXIAL_THETA, in_blk)
    col_t = _rope_tables(pos % GRID_W, half, AXIAL_THETA, in_blk)
    first = (lane < half)[None, :]
    b_tabs = tuple(jnp.where(first, r, c) for r, c in zip(row_t, col_t))
    return a_tabs + b_tabs


def _apply_rope(x, cos, s_lo, s_hi, h):
    return x * cos + pltpu.roll(x, LANES - h, 1) * s_lo + pltpu.roll(x, h, 1) * s_hi


def _attn_front_kernel(x_ref, shift_ref, scl_ref, pre_ref, w_ref, gq_ref, gk_ref, bd_ref,
                       ac_ref, alo_ref, ahi_ref, bc_ref, blo_ref, bhi_ref,
                       qa_ref, ka_ref, va_ref, qb_ref, kt_ref, vb_ref, g_ref):
    h = _mod_norm(x_ref[0], pre_ref[...], scl_ref[0], shift_ref[0]).astype(BF16)
    lane = lax.broadcasted_iota(jnp.int32, (1, LANES), 1)
    a_tabs = (ac_ref[...], alo_ref[...], ahi_ref[...])
    b_tabs = (bc_ref[...], blo_ref[...], bhi_ref[...])
    def head_rms(xc, g):
        n = xc.shape[1]
        bd = bd_ref[0:n, 0:n]
        sq = xc * xc
        hi = sq.astype(BF16)
        lo = (sq - hi.astype(F32)).astype(BF16)
        ss = _dot(hi, bd) + _dot(lo, bd)
        g = jnp.concatenate([g] * (n // LANES), axis=1)
        return xc * lax.rsqrt(ss * (1.0 / HEAD_DIM) + NORM_EPS) * g

    off = 0
    qa = _dot(h, w_ref[:, off:off + A_WIDTH]); off += A_WIDTH
    for c in range(A_WIDTH // LANES):
        col = _apply_rope(qa[:, c * LANES:(c + 1) * LANES], *a_tabs, ROPE_DIM // 2)
        qa_ref[0, :, c * LANES:(c + 1) * LANES] = (col * Q_SCALE).astype(BF16)
    ka = _dot(h, w_ref[:, off:off + A_WIDTH]); off += A_WIDTH
    for c in range(A_WIDTH // LANES):
        col = _apply_rope(ka[:, c * LANES:(c + 1) * LANES], *a_tabs, ROPE_DIM // 2)
        ka_ref[0, :, c * LANES:(c + 1) * LANES] = col.astype(BF16)
    va_ref[0] = _dot(h, w_ref[:, off:off + A_WIDTH]).astype(BF16); off += A_WIDTH

    qb = _dot(h, w_ref[:, off:off + B_WIDTH]); off += B_WIDTH
    wide = bd_ref.shape[0]
    for c2 in range(B_WIDTH // wide):
        blk = head_rms(qb[:, c2 * wide:(c2 + 1) * wide], gq_ref[...])
        for c in range(wide // LANES):
            col = _apply_rope(blk[:, c * LANES:(c + 1) * LANES], *b_tabs, HEAD_DIM // 4)
            dst = c2 * wide + c * LANES
            qb_ref[0, :, dst:dst + LANES] = (col * Q_SCALE).astype(BF16)

    kb = _dot(h, w_ref[:, off:off + B_KV_WIDTH]); off += B_KV_WIDTH
    kb = _apply_rope(head_rms(kb, gk_ref[...]), *b_tabs, HEAD_DIM // 4)
    kt = kb.T
    for j in range(B_KV_HEADS):
        kj = kt[j * HEAD_DIM:(j + 1) * HEAD_DIM].astype(BF16)
        kt_ref[0, j] = jnp.concatenate([kj, kj], axis=0)

    vb = _dot(h, w_ref[:, off:off + B_KV_WIDTH]); off += B_KV_WIDTH
    ones_col = jnp.where(lane == HEAD_DIM, 1.0, 0.0)
    for j in range(B_KV_HEADS):
        vj = vb if j == 0 else pltpu.roll(vb, HEAD_DIM, 1)
        vb_ref[0, j] = jnp.where(lane < HEAD_DIM, vj, ones_col).astype(BF16)

    g_ref[0] = _silu(_dot(h, w_ref[:, off:off + MIX_WIDTH])).astype(BF16)


def _attn_front(x, shift, scl, pre_g, w_in, gq, gk, bd, tabs, tm):
    b, s, d = x.shape
    n_in = w_in.shape[1]
    row = lambda i, j: (i, j, 0)
    per_b = lambda i, j: (i, 0, 0)
    const = lambda i, j: (0, 0)
    tab_spec = pl.BlockSpec((tm, LANES), lambda i, j: (j, 0))
    wide = lambda w: pl.BlockSpec((1, tm, w), row)
    return pl.pallas_call(
        _attn_front_kernel,
        grid=(b, s // tm),
        in_specs=[wide(d), pl.BlockSpec((1, 1, d), per_b), pl.BlockSpec((1, 1, d), per_b),
                  pl.BlockSpec((1, d), const), pl.BlockSpec((d, n_in), const),
                  pl.BlockSpec((1, LANES), const), pl.BlockSpec((1, LANES), const),
                  pl.BlockSpec(bd.shape, const)] + [tab_spec] * 6,
        out_specs=[wide(A_WIDTH), wide(A_WIDTH), wide(A_WIDTH), wide(B_WIDTH),
                   pl.BlockSpec((1, B_KV_HEADS, LANES, tm), lambda i, j: (i, 0, 0, j)),
                   pl.BlockSpec((1, B_KV_HEADS, tm, LANES), lambda i, j: (i, 0, j, 0)),
                   wide(MIX_WIDTH)],
        out_shape=[jax.ShapeDtypeStruct((b, s, A_WIDTH), BF16)] * 3
        + [jax.ShapeDtypeStruct((b, s, B_WIDTH), BF16),
           jax.ShapeDtypeStruct((b, B_KV_HEADS, LANES, s), BF16),
           jax.ShapeDtypeStruct((b, B_KV_HEADS, s, LANES), BF16),
           jax.ShapeDtypeStruct((b, s, MIX_WIDTH), BF16)],
        compiler_params=_params("parallel", "parallel"),
        name="attn_front",
    )(x, shift, scl, pre_g, w_in, gq, gk, bd, *tabs)


def _gqa_kernel(q_ref, kt_ref, v_ref, o_ref, *, tk):
    tq = q_ref.shape[1]
    s = kt_ref.shape[3]
    lane = lax.broadcasted_iota(jnp.int32, (1, LANES), 1)
    first = lane < HEAD_DIM
    zero = jnp.zeros((), BF16)
    parts = []
    for c in range(2):
        qc = q_ref[0, :, c * LANES:(c + 1) * LANES]
        parts += [jnp.where(first, qc, zero), jnp.where(first, zero, qc)]
    qs = jnp.concatenate(parts, axis=0)
    rows = qs.shape[0]

    def step(i, carry):
        m, acc = carry
        k0 = pl.multiple_of(i * tk, tk)
        sc = _dot(qs, kt_ref[0, 0, :, pl.ds(k0, tk)])
        m_new = jnp.maximum(m, jnp.max(sc, axis=1, keepdims=True))
        p = jnp.exp2(sc - m_new).astype(BF16)
        acc = acc * jnp.exp2(m - m_new) + _dot(p, v_ref[0, 0, pl.ds(k0, tk), :])
        return m_new, acc

    m0 = jnp.full((rows, 1), NEG_INF, F32)
    acc0 = jnp.zeros((rows, LANES), F32)
    _, acc = lax.fori_loop(0, s // tk, step, (m0, acc0), unroll=True)
    out = acc / acc[:, HEAD_DIM:HEAD_DIM + 1]
    for c in range(2):
        even = out[(2 * c) * tq:(2 * c + 1) * tq]
        odd = pltpu.roll(out[(2 * c + 1) * tq:(2 * c + 2) * tq], HEAD_DIM, 1)
        o_ref[0, :, c * LANES:(c + 1) * LANES] = jnp.where(first, even, odd).astype(BF16)


def _gqa_flash(qb, kt, vb, tq, tk):
    b, s, _ = qb.shape
    group_w = B_WIDTH // B_KV_HEADS
    return pl.pallas_call(
        functools.partial(_gqa_kernel, tk=tk),
        grid=(b, B_KV_HEADS, s // tq),
        in_specs=[pl.BlockSpec((1, tq, group_w), lambda i, j, t: (i, t, j)),
                  pl.BlockSpec((1, 1, LANES, s), lambda i, j, t: (i, j, 0, 0)),
                  pl.BlockSpec((1, 1, s, LANES), lambda i, j, t: (i, j, 0, 0))],
        out_specs=pl.BlockSpec((1, tq, group_w), lambda i, j, t: (i, t, j)),
        out_shape=jax.ShapeDtypeStruct((b, s, B_WIDTH), BF16),
        compiler_params=_params("parallel", "parallel", "arbitrary"),
        name="gqa_flash",
    )(qb, kt, vb)


A_CHUNK = 128
A_RADIUS = 64
A_GROUP = 32
A_PERM_BLOCK = MXU_DIM
_A_SORTED_DILATIONS = tuple(d for _, d in DILATED_PATTERNS if d != 1)


def _class_perm(dil):
    i = jnp.arange(A_PERM_BLOCK)
    m = A_PERM_BLOCK // dil
    src = (i % m) * dil + i // m
    return (src[:, None] == i[None, :]).astype(BF16)


def _dilated_kernel(start_ref, q_ref, k_ref, v_ref, p4_ref, p16_ref, o_ref, q0d, q1d, kd, vd, bias,
                    st_a, st_b, tile):
    perm_refs = dict(zip(_A_SORTED_DILATIONS, (p4_ref, p16_ref)))
    s = q_ref.shape[1]
    n_chunks = s // A_CHUNK
    kw = 2 * A_CHUNK
    lane = lax.broadcasted_iota(jnp.int32, (1, LANES), 1)
    first = lane < HEAD_DIM

    tail = kw - A_RADIUS
    kd[0:A_RADIUS, :] = jnp.zeros((A_RADIUS, LANES), BF16)
    kd[A_RADIUS + s:A_RADIUS + s + tail, :] = jnp.zeros((tail, LANES), BF16)
    vd[0:A_RADIUS, 0:LANES] = jnp.zeros((A_RADIUS, LANES), BF16)
    vd[A_RADIUS + s:A_RADIUS + s + tail, 0:LANES] = jnp.zeros((tail, LANES), BF16)
    vd[:, LANES:2 * LANES] = jnp.ones((vd.shape[0], LANES), BF16)

    ri = lax.broadcasted_iota(jnp.int32, (A_CHUNK, kw), 0)
    ci = lax.broadcasted_iota(jnp.int32, (A_CHUNK, kw), 1)
    band = (ci >= ri) & (ci <= ri + 2 * A_RADIUS)
    for at_start in (0, 1):
        for at_end in (0, 1):
            ok = band
            if at_start:
                ok = ok & (ci >= A_RADIUS)
            if at_end:
                ok = ok & (ci < A_RADIUS + A_CHUNK)
            bias[at_start + 2 * at_end] = jnp.where(ok, 0.0, NEG_INF)

    order = sorted(DILATED_PATTERNS, key=lambda wd: -wd[1])
    states = (st_a, st_b)
    parent_dil = None
    for p, (window, dil) in enumerate(order):
        assert window // (2 * dil) == A_RADIUS
        cls_len = s // dil
        cls_chunks = cls_len // A_CHUNK
        final = p == len(order) - 1
        assert (dil == 1) == final
        src_state, dst_state = states[(p + 1) % 2], states[p % 2]

        if dil == 1:
            qv = q_ref[0]
            q0d[...] = jnp.where(first, qv, jnp.zeros((), BF16))
            q1d[...] = jnp.where(first, jnp.zeros((), BF16), qv)
            kd[A_RADIUS:A_RADIUS + s, :] = k_ref[0]
            vd[A_RADIUS:A_RADIUS + s, 0:LANES] = v_ref[0]
        else:
            perm = perm_refs[dil][...]
            piece = A_PERM_BLOCK // dil

            def sort_block(bk, carry, dil=dil, cls_len=cls_len, perm=perm, piece=piece):
                r0 = pl.multiple_of(bk * A_PERM_BLOCK, A_PERM_BLOCK)
                qv = q_ref[0, pl.ds(r0, A_PERM_BLOCK), :]
                zero = jnp.zeros((), BF16)
                qm = jnp.concatenate([jnp.where(first, qv, zero), jnp.where(first, zero, qv)], axis=1)
                kv = jnp.concatenate([k_ref[0, pl.ds(r0, A_PERM_BLOCK), :],
                                      v_ref[0, pl.ds(r0, A_PERM_BLOCK), :]], axis=1)
                yq = _dot(perm, qm).astype(BF16)
                ykv = _dot(perm, kv).astype(BF16)
                for r in range(dil):
                    src = slice(r * piece, (r + 1) * piece)
                    row = pl.multiple_of(r * cls_len + bk * piece, piece)
                    q0d[pl.ds(row, piece), :] = yq[src, :LANES]
                    q1d[pl.ds(row, piece), :] = yq[src, LANES:]
                    kd[pl.ds(A_RADIUS + row, piece), :] = ykv[src, :LANES]
                    vd[pl.ds(A_RADIUS + row, piece), 0:LANES] = ykv[src, LANES:]
                return carry

            lax.fori_loop(0, s // A_PERM_BLOCK, sort_block, 0, unroll=True)

        def chunk(c, slot, dil=dil, cls_chunks=cls_chunks, final=final, parent_dil=parent_dil,
                  src_state=src_state, dst_state=dst_state):
            r0 = pl.multiple_of(c * A_CHUNK, A_CHUNK)
            qq = jnp.concatenate([q0d[pl.ds(r0, A_CHUNK), :], q1d[pl.ds(r0, A_CHUNK), :]], axis=0)
            kwin = kd[pl.ds(r0, kw), :]
            vwin = vd[pl.ds(r0, kw), :]
            sc = lax.dot_general(qq, kwin, (((1,), (1,)), ((), ())), preferred_element_type=F32)
            in_cls = c % cls_chunks
            mask = bias[jnp.where(in_cls == 0, 1, 0) + jnp.where(in_cls == cls_chunks - 1, 2, 0)]
            sc = sc + jnp.concatenate([mask, mask], axis=0)
            m = jnp.max(sc, axis=1, keepdims=True)
            pr = jnp.exp2(sc - m).astype(BF16)
            pv = _dot(pr, vwin)
            pick = lambda x: jnp.where(first, x[:A_CHUNK], x[A_CHUNK:])
            num, den = pick(pv[:, :LANES]), pick(pv[:, LANES:])
            top = pick(jnp.broadcast_to(m, (2 * A_CHUNK, LANES)))
            if parent_dil is not None:
                fan = parent_dil // dil
                piece = A_CHUNK // fan
                cls = c // cls_chunks
                for a in range(fan):
                    row = pl.multiple_of((cls + dil * a) * (s // parent_dil) + in_cls * piece, piece)
                    for k in range(3):
                        tile[slot, k, pl.ds(a, piece, stride=fan), :] = src_state[k, pl.ds(row, piece), :]
                top_p = tile[slot, 1]
                top_new = jnp.maximum(top, top_p)
                w_c, w_p = jnp.exp2(top - top_new), jnp.exp2(top_p - top_new)
                num = w_c * num + w_p * tile[slot, 0]
                den = w_c * den + w_p * tile[slot, 2]
                top = top_new
            if final:
                o_ref[0, pl.ds(r0, A_CHUNK), :] = (num * (1.0 / den)).astype(BF16)
            else:
                dst_state[0, pl.ds(r0, A_CHUNK), :] = num
                dst_state[1, pl.ds(r0, A_CHUNK), :] = top
                dst_state[2, pl.ds(r0, A_CHUNK), :] = den

        group = min(A_GROUP, n_chunks)
        assert n_chunks % group == 0

        def chunk_group(g, carry, chunk=chunk, group=group):
            for slot in range(group):
                chunk(g * group + slot, slot)
            return carry

        lax.fori_loop(start_ref[0], start_ref[0] + n_chunks // group, chunk_group, 0)
        parent_dil = dil


def _dilated_attn(qa, ka, va):
    b, s, w = qa.shape
    spec = pl.BlockSpec((1, s, LANES), lambda i, j: (i, 0, j))
    pad_rows = s + 2 * A_CHUNK
    perm_spec = pl.BlockSpec((A_PERM_BLOCK, A_PERM_BLOCK), lambda i, j: (0, 0))
    perms = [_class_perm(d) for d in _A_SORTED_DILATIONS]
    return pl.pallas_call(
        _dilated_kernel,
        grid=(b, w // LANES),
        in_specs=[pl.BlockSpec(memory_space=pltpu.SMEM), spec, spec, spec] + [perm_spec] * len(perms),
        out_specs=spec,
        out_shape=jax.ShapeDtypeStruct((b, s, w), BF16),
        scratch_shapes=[pltpu.VMEM((s, LANES), BF16),
                        pltpu.VMEM((s, LANES), BF16),
                        pltpu.VMEM((pad_rows, LANES), BF16),
                        pltpu.VMEM((pad_rows, 2 * LANES), BF16),
                        pltpu.VMEM((4, A_CHUNK, 2 * A_CHUNK), F32),
                        pltpu.VMEM((3, s, LANES), F32),
                        pltpu.VMEM((3, s, LANES), F32),
                        pltpu.VMEM((A_GROUP, 3, A_CHUNK, LANES), F32)],
        compiler_params=_params("parallel", "parallel"),
        name="dilated_attn",
    )(jnp.zeros((1,), jnp.int32), qa, ka, va, *perms)


def _attn_mix(x, oa, ob, g, gate, post_g, w_ref):
    ya = (oa.astype(F32) * g[:, :A_WIDTH].astype(F32)).astype(BF16)
    yb = (ob.astype(F32) * g[:, A_WIDTH:].astype(F32)).astype(BF16)
    m = _dot(ya, w_ref[:A_WIDTH, :]) + _dot(yb, w_ref[A_WIDTH:, :])
    return _post_residual(x, m, post_g, gate)


def _attn_back_kernel(x_ref, oa_ref, ob_ref, g_ref, gate_ref, post_ref, w_ref, o_ref):
    o_ref[0] = _attn_mix(x_ref[0], oa_ref[0], ob_ref[0], g_ref[0], gate_ref[0], post_ref[...], w_ref)


def _attn_back(x, oa, ob, g, gate, post_g, w_out, tm):
    b, s, d = x.shape
    row = lambda i, j: (i, j, 0)
    per_b = lambda i, j: (i, 0, 0)
    const = lambda i, j: (0, 0)
    return pl.pallas_call(
        _attn_back_kernel,
        grid=(b, s // tm),
        in_specs=[pl.BlockSpec((1, tm, d), row), pl.BlockSpec((1, tm, A_WIDTH), row),
                  pl.BlockSpec((1, tm, B_WIDTH), row), pl.BlockSpec((1, tm, MIX_WIDTH), row),
                  pl.BlockSpec((1, 1, d), per_b), pl.BlockSpec((1, d), const),
                  pl.BlockSpec((MIX_WIDTH, d), const)],
        out_specs=pl.BlockSpec((1, tm, d), row),
        out_shape=jax.ShapeDtypeStruct((b, s, d), F32),
        compiler_params=_params("parallel", "parallel"),
        name="attn_back",
    )(x, oa, ob, g, gate, post_g, w_out)


POOL_HALO = SUBLANES
POOL_TAIL = 2 * SUBLANES
BF16_TILE_ROWS = 2 * SUBLANES


def _with_halo(prev_ref, main_ref, next_ref):
    prev = prev_ref[0].astype(F32)
    return jnp.concatenate([prev[prev.shape[0] - POOL_HALO:], main_ref[0].astype(F32),
                            next_ref[0].astype(F32)[:POOL_HALO]], axis=0)


def _back_pool_kernel(xp_ref, x_ref, xn_ref, oap_ref, oa_ref, oan_ref, obp_ref, ob_ref, obn_ref,
                      gp_ref, g_ref, gn_ref, gate0_ref, post0_ref, w_out0_ref,
                      shift_ref, scl_ref, gate_ref, pre_ref, post_ref,
                      w_in_ref, w_grp_ref, scale_ref, w_out_ref, o_ref, u_scr, c_a, c_b, *, seq):
    tm = x_ref.shape[1]
    width = scale_ref.shape[1]
    gdim = width // POOL_GROUPS
    ext = tm + 2 * POOL_HALO
    i = pl.program_id(1)
    xe = _attn_mix(_with_halo(xp_ref, x_ref, xn_ref), _with_halo(oap_ref, oa_ref, oan_ref),
                   _with_halo(obp_ref, ob_ref, obn_ref), _with_halo(gp_ref, g_ref, gn_ref),
                   gate0_ref[0], post0_ref[...], w_out0_ref)
    x = xe[POOL_HALO:POOL_HALO + tm]
    he = _mod_norm(xe, pre_ref[...], scl_ref[0], shift_ref[0]).astype(BF16)
    t_ext = i * tm - POOL_HALO + lax.broadcasted_iota(jnp.int32, (ext, 1), 0)
    u_ext = _dot(he, w_in_ref[:, :width])
    u_scr[0:ext, :] = jnp.where((t_ext >= 0) & (t_ext < seq), u_ext, 0.0)
    u_scr[ext:ext + POOL_TAIL, :] = jnp.zeros((POOL_TAIL, width), F32)
    gate_act = _silu(_dot(he[POOL_HALO:POOL_HALO + tm], w_in_ref[:, width:]))
    t = i * tm + lax.broadcasted_iota(jnp.int32, (tm, 1), 0)
    m = jnp.zeros((tm, o_ref.shape[2]), F32)
    for g, window in enumerate(POOL_WINDOWS):
        half = window // 2
        assert half <= POOL_HALO
        cols = slice(g * gdim, (g + 1) * gdim)
        read = lambda e, n, cols=cols: u_scr[e:e + n, cols]
        span, n, bufs = 1, ext + POOL_TAIL, (c_a, c_b)
        while span < half:
            n -= SUBLANES
            dst = bufs[0]
            dst[0:n, :] = read(0, n) + read(span, n)
            read = lambda e, n, dst=dst: dst[e:e + n, :]
            span, bufs = 2 * span, bufs[::-1]
        win = read(POOL_HALO - half, tm) + read(POOL_HALO, tm)
        cnt = (jnp.minimum(t + half, seq) - jnp.maximum(t - half, 0)).astype(F32)
        pooled = win / cnt - u_scr[POOL_HALO:POOL_HALO + tm, cols]
        mixed = _dot(pooled.astype(BF16), w_grp_ref[g]) * scale_ref[:, cols]
        y = (mixed * gate_act[:, cols]).astype(BF16)
        m = m + _dot(y, w_out_ref[cols, :])
    o_ref[0] = _post_residual(x, m, post_ref[...], gate_ref[0])


def _back_pool_layer(x, oa, ob, g, gate0, post0_g, w_out0, shift, scl, gate, pre_g, post_g,
                     w_in, w_grp, scale, w_out, tm):
    b, s, d = x.shape
    width = scale.shape[1]
    row = lambda i, j: (i, j, 0)
    per_b = lambda i, j: (i, 0, 0)
    const = lambda i, j: (0, 0)

    def with_halo(cols, halo_rows):
        nt, last = tm // halo_rows, s // halo_rows - 1
        return [pl.BlockSpec((1, halo_rows, cols), lambda i, j: (i, jnp.maximum(j * nt - 1, 0), 0)),
                pl.BlockSpec((1, tm, cols), row),
                pl.BlockSpec((1, halo_rows, cols), lambda i, j: (i, jnp.minimum((j + 1) * nt, last), 0))]

    scratch_rows = tm + 2 * POOL_HALO + POOL_TAIL
    return pl.pallas_call(
        functools.partial(_back_pool_kernel, seq=s),
        grid=(b, s // tm),
        in_specs=with_halo(d, POOL_HALO) + with_halo(A_WIDTH, BF16_TILE_ROWS)
        + with_halo(B_WIDTH, BF16_TILE_ROWS) + with_halo(MIX_WIDTH, BF16_TILE_ROWS)
        + [pl.BlockSpec((1, 1, d), per_b), pl.BlockSpec((1, d), const), pl.BlockSpec((MIX_WIDTH, d), const),
           pl.BlockSpec((1, 1, d), per_b), pl.BlockSpec((1, 1, d), per_b), pl.BlockSpec((1, 1, d), per_b),
           pl.BlockSpec((1, d), const), pl.BlockSpec((1, d), const), pl.BlockSpec((d, 2 * width), const),
           pl.BlockSpec(w_grp.shape, lambda i, j: (0, 0, 0)),
           pl.BlockSpec((1, width), const), pl.BlockSpec((width, d), const)],
        out_specs=pl.BlockSpec((1, tm, d), row),
        out_shape=jax.ShapeDtypeStruct((b, s, d), F32),
        scratch_shapes=[pltpu.VMEM((scratch_rows, width), F32),
                        pltpu.VMEM((scratch_rows, width // POOL_GROUPS), F32),
                        pltpu.VMEM((scratch_rows, width // POOL_GROUPS), F32)],
        compiler_params=_params("parallel", "parallel"),
        name="back_pool_layer",
    )(x, x, x, oa, oa, oa, ob, ob, ob, g, g, g, gate0, post0_g, w_out0,
      shift, scl, gate, pre_g, post_g, w_in, w_grp, scale, w_out)


def _trunk(x, mods, pre_norm, post_norm, attn_w_in, attn_q_norm, attn_k_norm, attn_w_out,
           pool_w_in, pool_w_grp, pool_scale, pool_w_out, tabs, bd, *, tm, tq, tk):
    depth = pre_norm.shape[0]
    d = x.shape[-1]

    def mod(l):
        return tuple(mods[l][:, None, j * d:(j + 1) * d] for j in range(3))

    for l in range(0, depth, 2):
        i = l // 2
        shift, scl, gate = mod(l)
        gq = jnp.tile(attn_q_norm[i], LANES // HEAD_DIM)[None, :]
        gk = jnp.tile(attn_k_norm[i], LANES // HEAD_DIM)[None, :]
        qa, ka, va, qb, kt, vb, g = _attn_front(x, shift, scl, pre_norm[l][None, :], attn_w_in[i], gq, gk,
                                                bd, tabs, tm)
        oa = _dilated_attn(qa, ka, va)
        ob = _gqa_flash(qb, kt, vb, tq, tk)
        back = (x, oa, ob, g, gate, post_norm[l][None, :], attn_w_out[i])
        if l + 1 < depth:
            shift1, scl1, gate1 = mod(l + 1)
            x = _back_pool_layer(*back, shift1, scl1, gate1, pre_norm[l + 1][None, :],
                                 post_norm[l + 1][None, :], pool_w_in[i], pool_w_grp[i],
                                 pool_scale[i][None, :], pool_w_out[i], tm)
        else:
            x = _attn_back(*back, tm)
    return x


def kernel(x_prompt, x_sample, c_prompt, c_sample, ada_w, ada_b, pre_norm, post_norm, attn_w_in, attn_q_norm, attn_k_norm, attn_w_out, pool_w_in, pool_w_grp, pool_scale, pool_w_out):
    nb_p = x_prompt.shape[0]
    mods = _ada_mod(jnp.concatenate([c_prompt, c_sample], axis=0), ada_w, ada_b)
    head_of_lane = jnp.arange(MXU_DIM) // HEAD_DIM
    bd = (head_of_lane[:, None] == head_of_lane[None, :]).astype(BF16)
    weights = (pre_norm, post_norm, attn_w_in.astype(BF16), attn_q_norm, attn_k_norm,
               attn_w_out.astype(BF16), pool_w_in.astype(BF16), pool_w_grp.astype(BF16), pool_scale,
               pool_w_out.astype(BF16))
    outs = []
    for x, sl in ((x_prompt, slice(0, nb_p)), (x_sample, slice(nb_p, None))):
        tabs = _make_tables(x.shape[1])
        outs.append(_trunk(x, mods[:, sl], *weights, tabs, bd, tm=512, tq=512, tk=512))
    return tuple(outs)
```

```python
import functools
import math

import jax
import jax.numpy as jnp
from jax import lax
from jax.experimental import pallas as pl
from jax.experimental.pallas import tpu as pltpu

HEAD_DIM = 64
A_HEADS = 8
B_HEADS = 8
B_KV_HEADS = 2
A_WIDTH = A_HEADS * HEAD_DIM
B_WIDTH = B_HEADS * HEAD_DIM
B_KV_WIDTH = B_KV_HEADS * HEAD_DIM
MIX_WIDTH = A_WIDTH + B_WIDTH
DILATED_PATTERNS = ((128, 1), (512, 4), (2048, 16))
ROPE_THETA = 500000.0
ROPE_DIM = HEAD_DIM // 4
AXIAL_THETA = 10000.0
GRID_W = 64
POOL_WINDOWS = (2, 4, 8, 16)
POOL_GROUPS = 4
NORM_EPS = 1e-6
NEG_INF = -1e30

LANES = 128
SUBLANES = 8
MXU_DIM = 256
VMEM_LIMIT_BYTES = 56 * 1024 * 1024
Q_SCALE = math.log2(math.e) / math.sqrt(HEAD_DIM)

BF16 = jnp.bfloat16
F32 = jnp.float32


def _params(*semantics):
    return pltpu.CompilerParams(dimension_semantics=semantics, vmem_limit_bytes=VMEM_LIMIT_BYTES)


def _dot(a, b):
    return jnp.dot(a, b, preferred_element_type=F32)


def _silu(x):
    return x * (1.0 / (1.0 + jnp.exp(-x)))


def _mod_norm(x, pre_g, scl, shift):
    ms = jnp.mean(x * x, axis=-1, keepdims=True)
    return x * lax.rsqrt(ms + NORM_EPS) * (pre_g * (1.0 + scl)) + shift


def _post_residual(x, m, post_g, gate):
    ms = jnp.mean(m * m, axis=-1, keepdims=True)
    return x + m * lax.rsqrt(ms + NORM_EPS) * (gate * post_g)


def _ada_kernel(c_ref, w_ref, b_ref, o_ref):
    a = _silu(c_ref[...]).astype(BF16)
    o_ref[0] = _dot(a, w_ref[0].astype(BF16)) + b_ref[0]


def _ada_mod(c_all, ada_w, ada_b):
    depth, d, d3 = ada_w.shape
    nb = c_all.shape[0]
    tn = d
    return pl.pallas_call(
        _ada_kernel,
        grid=(depth, d3 // tn),
        in_specs=[pl.BlockSpec((nb, d), lambda l, j: (0, 0)),
                  pl.BlockSpec((1, d, tn), lambda l, j: (l, 0, j)),
                  pl.BlockSpec((1, 1, tn), lambda l, j: (l, 0, j))],
        out_specs=pl.BlockSpec((1, nb, tn), lambda l, j: (l, 0, j)),
        out_shape=jax.ShapeDtypeStruct((depth, nb, d3), F32),
        compiler_params=_params("arbitrary", "arbitrary"),
        name="ada_mod",
    )(c_all, ada_w, ada_b.reshape(depth, 1, d3))


def _rope_tables(pos, rot_dim, theta, lane_in_block):
    h = rot_dim // 2
    f = jnp.where(lane_in_block >= 0, lane_in_block % h, 0)
    inv = theta ** (-(2.0 * f.astype(F32)) / rot_dim)
    ang = pos.astype(F32)[:, None] * inv[None, :]
    active = (lane_in_block >= 0)[None, :]
    lo = (lane_in_block < h)[None, :] & active
    hi = (lane_in_block >= h)[None, :] & active
    cos = jnp.where(active, jnp.cos(ang), 1.0)
    sin = jnp.sin(ang)
    return cos, jnp.where(lo, -sin, 0.0), jnp.where(hi, sin, 0.0)


def _make_tables(s):
    lane = jnp.arange(LANES) % HEAD_DIM
    pos = jnp.arange(s)
    a_tabs = _rope_tables(pos, ROPE_DIM, ROPE_THETA, jnp.where(lane < ROPE_DIM, lane, -1))
    half = HEAD_DIM // 2
    in_blk = lane % half
    row_t = _rope_tables(pos // GRID_W, half, AXIAL_THETA, in_blk)
    col_t = _rope_tables(pos % GRID_W, half, AXIAL_THETA, in_blk)
    first = (lane < half)[None, :]
    b_tabs = tuple(jnp.where(first, r, c) for r, c in zip(row_t, col_t))
    return a_tabs + b_tabs


def _apply_rope(x, cos, s_lo, s_hi, h):
    return x * cos + pltpu.roll(x, LANES - h, 1) * s_lo + pltpu.roll(x, h, 1) * s_hi


def _attn_front_kernel(x_ref, shift_ref, scl_ref, pre_ref, w_ref, gq_ref, gk_ref, bd_ref,
                       ac_ref, alo_ref, ahi_ref, bc_ref, blo_ref, bhi_ref,
                       qa_ref, ka_ref, va_ref, qb_ref, kt_ref, vb_ref, g_ref):
    h = _mod_norm(x_ref[0], pre_ref[...], scl_ref[0], shift_ref[0]).astype(BF16)
    lane = lax.broadcasted_iota(jnp.int32, (1, LANES), 1)
    a_tabs = (ac_ref[...], alo_ref[...], ahi_ref[...])
    b_tabs = (bc_ref[...], blo_ref[...], bhi_ref[...])
    def head_rms(xc, g):
        n = xc.shape[1]
        bd = bd_ref[0:n, 0:n]
        sq = xc * xc
        hi = sq.astype(BF16)
        lo = (sq - hi.astype(F32)).astype(BF16)
        ss = _dot(hi, bd) + _dot(lo, bd)
        g = jnp.concatenate([g] * (n // LANES), axis=1)
        return xc * lax.rsqrt(ss * (1.0 / HEAD_DIM) + NORM_EPS) * g

    off = 0
    qa = _dot(h, w_ref[:, off:off + A_WIDTH]); off += A_WIDTH
    for c in range(A_WIDTH // LANES):
        col = _apply_rope(qa[:, c * LANES:(c + 1) * LANES], *a_tabs, ROPE_DIM // 2)
        qa_ref[0, :, c * LANES:(c + 1) * LANES] = (col * Q_SCALE).astype(BF16)
    ka = _dot(h, w_ref[:, off:off + A_WIDTH]); off += A_WIDTH
    for c in range(A_WIDTH // LANES):
        col = _apply_rope(ka[:, c * LANES:(c + 1) * LANES], *a_tabs, ROPE_DIM // 2)
        ka_ref[0, :, c * LANES:(c + 1) * LANES] = col.astype(BF16)
    va_ref[0] = _dot(h, w_ref[:, off:off + A_WIDTH]).astype(BF16); off += A_WIDTH

    qb = _dot(h, w_ref[:, off:off + B_WIDTH]); off += B_WIDTH
    wide = bd_ref.shape[0]
    for c2 in range(B_WIDTH // wide):
        blk = head_rms(qb[:, c2 * wide:(c2 + 1) * wide], gq_ref[...])
        for c in range(wide // LANES):
            col = _apply_rope(blk[:, c * LANES:(c + 1) * LANES], *b_tabs, HEAD_DIM // 4)
            dst = c2 * wide + c * LANES
            qb_ref[0, :, dst:dst + LANES] = (col * Q_SCALE).astype(BF16)

    kb = _dot(h, w_ref[:, off:off + B_KV_WIDTH]); off += B_KV_WIDTH
    kb = _apply_rope(head_rms(kb, gk_ref[...]), *b_tabs, HEAD_DIM // 4)
    kt = kb.T
    for j in range(B_KV_HEADS):
        kj = kt[j * HEAD_DIM:(j + 1) * HEAD_DIM].astype(BF16)
        kt_ref[0, j] = jnp.concatenate([kj, kj], axis=0)

    vb = _dot(h, w_ref[:, off:off + B_KV_WIDTH]); off += B_KV_WIDTH
    ones_col = jnp.where(lane == HEAD_DIM, 1.0, 0.0)
    for j in range(B_KV_HEADS):
        vj = vb if j == 0 else pltpu.roll(vb, HEAD_DIM, 1)
        vb_ref[0, j] = jnp.where(lane < HEAD_DIM, vj, ones_col).astype(BF16)

    g_ref[0] = _silu(_dot(h, w_ref[:, off:off + MIX_WIDTH])).astype(BF16)


def _attn_front(x, shift, scl, pre_g, w_in, gq, gk, bd, tabs, tm):
    b, s, d = x.shape
    n_in = w_in.shape[1]
    row = lambda i, j: (i, j, 0)
    per_b = lambda i, j: (i, 0, 0)
    const = lambda i, j: (0, 0)
    tab_spec = pl.BlockSpec((tm, LANES), lambda i, j: (j, 0))
    wide = lambda w: pl.BlockSpec((1, tm, w), row)
    return pl.pallas_call(
        _attn_front_kernel,
        grid=(b, s // tm),
        in_specs=[wide(d), pl.BlockSpec((1, 1, d), per_b), pl.BlockSpec((1, 1, d), per_b),
                  pl.BlockSpec((1, d), const), pl.BlockSpec((d, n_in), const),
                  pl.BlockSpec((1, LANES), const), pl.BlockSpec((1, LANES), const),
                  pl.BlockSpec(bd.shape, const)] + [tab_spec] * 6,
        out_specs=[wide(A_WIDTH), wide(A_WIDTH), wide(A_WIDTH), wide(B_WIDTH),
                   pl.BlockSpec((1, B_KV_HEADS, LANES, tm), lambda i, j: (i, 0, 0, j)),
                   pl.BlockSpec((1, B_KV_HEADS, tm, LANES), lambda i, j: (i, 0, j, 0)),
                   wide(MIX_WIDTH)],
        out_shape=[jax.ShapeDtypeStruct((b, s, A_WIDTH), BF16)] * 3
        + [jax.ShapeDtypeStruct((b, s, B_WIDTH), BF16),
           jax.ShapeDtypeStruct((b, B_KV_HEADS, LANES, s), BF16),
           jax.ShapeDtypeStruct((b, B_KV_HEADS, s, LANES), BF16),
           jax.ShapeDtypeStruct((b, s, MIX_WIDTH), BF16)],
        compiler_params=_params("parallel", "parallel"),
        name="attn_front",
    )(x, shift, scl, pre_g, w_in, gq, gk, bd, *tabs)


def _gqa_kernel(q_ref, kt_ref, v_ref, o_ref, *, tk):
    tq = q_ref.shape[1]
    s = kt_ref.shape[3]
    lane = lax.broadcasted_iota(jnp.int32, (1, LANES), 1)
    first = lane < HEAD_DIM
    zero = jnp.zeros((), BF16)
    parts = []
    for c in range(2):
        qc = q_ref[0, :, c * LANES:(c + 1) * LANES]
        parts += [jnp.where(first, qc, zero), jnp.where(first, zero, qc)]
    qs = jnp.concatenate(parts, axis=0)
    rows = qs.shape[0]

    def step(i, carry):
        m, acc = carry
        k0 = pl.multiple_of(i * tk, tk)
        sc = _dot(qs, kt_ref[0, 0, :, pl.ds(k0, tk)])
        m_new = jnp.maximum(m, jnp.max(sc, axis=1, keepdims=True))
        p = jnp.exp2(sc - m_new).astype(BF16)
        acc = acc * jnp.exp2(m - m_new) + _dot(p, v_ref[0, 0, pl.ds(k0, tk), :])
        return m_new, acc

    m0 = jnp.full((rows, 1), NEG_INF, F32)
    acc0 = jnp.zeros((rows, LANES), F32)
    _, acc = lax.fori_loop(0, s // tk, step, (m0, acc0), unroll=True)
    out = acc / acc[:, HEAD_DIM:HEAD_DIM + 1]
    for c in range(2):
        even = out[(2 * c) * tq:(2 * c + 1) * tq]
        odd = pltpu.roll(out[(2 * c + 1) * tq:(2 * c + 2) * tq], HEAD_DIM, 1)
        o_ref[0, :, c * LANES:(c + 1) * LANES] = jnp.where(first, even, odd).astype(BF16)


def _gqa_flash(qb, kt, vb, tq, tk):
    b, s, _ = qb.shape
    group_w = B_WIDTH // B_KV_HEADS
    return pl.pallas_call(
        functools.partial(_gqa_kernel, tk=tk),
        grid=(b, B_KV_HEADS, s // tq),
        in_specs=[pl.BlockSpec((1, tq, group_w), lambda i, j, t: (i, t, j)),
                  pl.BlockSpec((1, 1, LANES, s), lambda i, j, t: (i, j, 0, 0)),
                  pl.BlockSpec((1, 1, s, LANES), lambda i, j, t: (i, j, 0, 0))],
        out_specs=pl.BlockSpec((1, tq, group_w), lambda i, j, t: (i, t, j)),
        out_shape=jax.ShapeDtypeStruct((b, s, B_WIDTH), BF16),
        compiler_params=_params("parallel", "parallel", "arbitrary"),
        name="gqa_flash",
    )(qb, kt, vb)


A_CHUNK = 128
A_RADIUS = 64
A_GROUP = 32
A_PERM_BLOCK = MXU_DIM
_A_SORTED_DILATIONS = tuple(d for _, d in DILATED_PATTERNS if d != 1)


def _class_perm(dil):
    i = jnp.arange(A_PERM_BLOCK)
    m = A_PERM_BLOCK // dil
    src = (i % m) * dil + i // m
    return (src[:, None] == i[None, :]).astype(BF16)


def _dilated_kernel(start_ref, q_ref, k_ref, v_ref, p4_ref, p16_ref, o_ref, q0d, q1d, kd, vd, bias,
                    st_a, st_b, tile):
    perm_refs = dict(zip(_A_SORTED_DILATIONS, (p4_ref, p16_ref)))
    s = q_ref.shape[1]
    n_chunks = s // A_CHUNK
    kw = 2 * A_CHUNK
    lane = lax.broadcasted_iota(jnp.int32, (1, LANES), 1)
    first = lane < HEAD_DIM

    tail = kw - A_RADIUS
    kd[0:A_RADIUS, :] = jnp.zeros((A_RADIUS, LANES), BF16)
    kd[A_RADIUS + s:A_RADIUS + s + tail, :] = jnp.zeros((tail, LANES), BF16)
    vd[0:A_RADIUS, 0:LANES] = jnp.zeros((A_RADIUS, LANES), BF16)
    vd[A_RADIUS + s:A_RADIUS + s + tail, 0:LANES] = jnp.zeros((tail, LANES), BF16)
    vd[:, LANES:2 * LANES] = jnp.ones((vd.shape[0], LANES), BF16)

    ri = lax.broadcasted_iota(jnp.int32, (A_CHUNK, kw), 0)
    ci = lax.broadcasted_iota(jnp.int32, (A_CHUNK, kw), 1)
    band = (ci >= ri) & (ci <= ri + 2 * A_RADIUS)
    for at_start in (0, 1):
        for at_end in (0, 1):
            ok = band
            if at_start:
                ok = ok & (ci >= A_RADIUS)
            if at_end:
                ok = ok & (ci < A_RADIUS + A_CHUNK)
            bias[at_start + 2 * at_end] = jnp.where(ok, 0.0, NEG_INF)

    order = sorted(DILATED_PATTERNS, key=lambda wd: -wd[1])
    states = (st_a, st_b)
    parent_dil = None
    for p, (window, dil) in enumerate(order):
        assert window // (2 * dil) == A_RADIUS
        cls_len = s // dil
        cls_chunks = cls_len // A_CHUNK
        final = p == len(order) - 1
        assert (dil == 1) == final
        src_state, dst_state = states[(p + 1) % 2], states[p % 2]

        if dil == 1:
            qv = q_ref[0]
            q0d[...] = jnp.where(first, qv, jnp.zeros((), BF16))
            q1d[...] = jnp.where(first, jnp.zeros((), BF16), qv)
            kd[A_RADIUS:A_RADIUS + s, :] = k_ref[0]
            vd[A_RADIUS:A_RADIUS + s, 0:LANES] = v_ref[0]
        else:
            perm = perm_refs[dil][...]
            piece = A_PERM_BLOCK // dil

            def sort_block(bk, carry, dil=dil, cls_len=cls_len, perm=perm, piece=piece):
                r0 = pl.multiple_of(bk * A_PERM_BLOCK, A_PERM_BLOCK)
                qv = q_ref[0, pl.ds(r0, A_PERM_BLOCK), :]
                zero = jnp.zeros((), BF16)
                qm = jnp.concatenate([jnp.where(first, qv, zero), jnp.where(first, zero, qv)], axis=1)
                kv = jnp.concatenate([k_ref[0, pl.ds(r0, A_PERM_BLOCK), :],
                                      v_ref[0, pl.ds(r0, A_PERM_BLOCK), :]], axis=1)
                yq = _dot(perm, qm).astype(BF16)
                ykv = _dot(perm, kv).astype(BF16)
                for r in range(dil):
                    src = slice(r * piece, (r + 1) * piece)
                    row = pl.multiple_of(r * cls_len + bk * piece, piece)
                    q0d[pl.ds(row, piece), :] = yq[src, :LANES]
                    q1d[pl.ds(row, piece), :] = yq[src, LANES:]
                    kd[pl.ds(A_RADIUS + row, piece), :] = ykv[src, :LANES]
                    vd[pl.ds(A_RADIUS + row, piece), 0:LANES] = ykv[src, LANES:]
                return carry

            lax.fori_loop(0, s // A_PERM_BLOCK, sort_block, 0, unroll=True)

        def chunk(c, slot, dil=dil, cls_chunks=cls_chunks, final=final, parent_dil=parent_dil,
                  src_state=src_state, dst_state=dst_state):
            r0 = pl.multiple_of(c * A_CHUNK, A_CHUNK)
            qq = jnp.concatenate([q0d[pl.ds(r0, A_CHUNK), :], q1d[pl.ds(r0, A_CHUNK), :]], axis=0)
            kwin = kd[pl.ds(r0, kw), :]
            vwin = vd[pl.ds(r0, kw), :]
            sc = lax.dot_general(qq, kwin, (((1,), (1,)), ((), ())), preferred_element_type=F32)
            in_cls = c % cls_chunks
            mask = bias[jnp.where(in_cls == 0, 1, 0) + jnp.where(in_cls == cls_chunks - 1, 2, 0)]
            sc = sc + jnp.concatenate([mask, mask], axis=0)
            m = jnp.max(sc, axis=1, keepdims=True)
            pr = jnp.exp2(sc - m).astype(BF16)
            pv = _dot(pr, vwin)
            pick = lambda x: jnp.where(first, x[:A_CHUNK], x[A_CHUNK:])
            num, den = pick(pv[:, :LANES]), pick(pv[:, LANES:])
            top = pick(jnp.broadcast_to(m, (2 * A_CHUNK, LANES)))
            if parent_dil is not None:
                fan = parent_dil // dil
                piece = A_CHUNK // fan
                cls = c // cls_chunks
                for a in range(fan):
                    row = pl.multiple_of((cls + dil * a) * (s // parent_dil) + in_cls * piece, piece)
                    for k in range(3):
                        tile[slot, k, pl.ds(a, piece, stride=fan), :] = src_state[k, pl.ds(row, piece), :]
                top_p = tile[slot, 1]
                top_new = jnp.maximum(top, top_p)
                w_c, w_p = jnp.exp2(top - top_new), jnp.exp2(top_p - top_new)
                num = w_c * num + w_p * tile[slot, 0]
                den = w_c * den + w_p * tile[slot, 2]
                top = top_new
            if final:
                o_ref[0, pl.ds(r0, A_CHUNK), :] = (num * (1.0 / den)).astype(BF16)
            else:
                dst_state[0, pl.ds(r0, A_CHUNK), :] = num
                dst_state[1, pl.ds(r0, A_CHUNK), :] = top
                dst_state[2, pl.ds(r0, A_CHUNK), :] = den

        group = min(A_GROUP, n_chunks)
        assert n_chunks % group == 0

        def chunk_group(g, carry, chunk=chunk, group=group):
            for slot in range(group):
                chunk(g * group + slot, slot)
            return carry

        lax.fori_loop(start_ref[0], start_ref[0] + n_chunks // group, chunk_group, 0)
        parent_dil = dil


def _dilated_attn(qa, ka, va):
    b, s, w = qa.shape
    spec = pl.BlockSpec((1, s, LANES), lambda i, j: (i, 0, j))
    pad_rows = s + 2 * A_CHUNK
    perm_spec = pl.BlockSpec((A_PERM_BLOCK, A_PERM_BLOCK), lambda i, j: (0, 0))
    perms = [_class_perm(d) for d in _A_SORTED_DILATIONS]
    return pl.pallas_call(
        _dilated_kernel,
        grid=(b, w // LANES),
        in_specs=[pl.BlockSpec(memory_space=pltpu.SMEM), spec, spec, spec] + [perm_spec] * len(perms),
        out_specs=spec,
        out_shape=jax.ShapeDtypeStruct((b, s, w), BF16),
        scratch_shapes=[pltpu.VMEM((s, LANES), BF16),
                        pltpu.VMEM((s, LANES), BF16),
                        pltpu.VMEM((pad_rows, LANES), BF16),
                        pltpu.VMEM((pad_rows, 2 * LANES), BF16),
                        pltpu.VMEM((4, A_CHUNK, 2 * A_CHUNK), F32),
                        pltpu.VMEM((3, s, LANES), F32),
                        pltpu.VMEM((3, s, LANES), F32),
                        pltpu.VMEM((A_GROUP, 3, A_CHUNK, LANES), F32)],
        compiler_params=_params("parallel", "parallel"),
        name="dilated_attn",
    )(jnp.zeros((1,), jnp.int32), qa, ka, va, *perms)


def _attn_mix(x, oa, ob, g, gate, post_g, w_ref):
    ya = (oa.astype(F32) * g[:, :A_WIDTH].astype(F32)).astype(BF16)
    yb = (ob.astype(F32) * g[:, A_WIDTH:].astype(F32)).astype(BF16)
    m = _dot(ya, w_ref[:A_WIDTH, :]) + _dot(yb, w_ref[A_WIDTH:, :])
    return _post_residual(x, m, post_g, gate)


def _attn_back_kernel(x_ref, oa_ref, ob_ref, g_ref, gate_ref, post_ref, w_ref, o_ref):
    o_ref[0] = _attn_mix(x_ref[0], oa_ref[0], ob_ref[0], g_ref[0], gate_ref[0], post_ref[...], w_ref)


def _attn_back(x, oa, ob, g, gate, post_g, w_out, tm):
    b, s, d = x.shape
    row = lambda i, j: (i, j, 0)
    per_b = lambda i, j: (i, 0, 0)
    const = lambda i, j: (0, 0)
    return pl.pallas_call(
        _attn_back_kernel,
        grid=(b, s // tm),
        in_specs=[pl.BlockSpec((1, tm, d), row), pl.BlockSpec((1, tm, A_WIDTH), row),
                  pl.BlockSpec((1, tm, B_WIDTH), row), pl.BlockSpec((1, tm, MIX_WIDTH), row),
                  pl.BlockSpec((1, 1, d), per_b), pl.BlockSpec((1, d), const),
                  pl.BlockSpec((MIX_WIDTH, d), const)],
        out_specs=pl.BlockSpec((1, tm, d), row),
        out_shape=jax.ShapeDtypeStruct((b, s, d), F32),
        compiler_params=_params("parallel", "parallel"),
        name="attn_back",
    )(x, oa, ob, g, gate, post_g, w_out)


POOL_HALO = SUBLANES
POOL_TAIL = 2 * SUBLANES
BF16_TILE_ROWS = 2 * SUBLANES
BACK_ROW_BLOCKS = 3


def _with_halo(prev_ref, main_ref, next_ref):
    prev = prev_ref[0].astype(F32)
    return jnp.concatenate([prev[prev.shape[0] - POOL_HALO:], main_ref[0].astype(F32),
                            next_ref[0].astype(F32)[:POOL_HALO]], axis=0)


def _back_pool_kernel(xp_ref, x_ref, xn_ref, oap_ref, oa_ref, oan_ref, obp_ref, ob_ref, obn_ref,
                      gp_ref, g_ref, gn_ref, gate0_ref, post0_ref, w_out0_ref,
                      shift_ref, scl_ref, gate_ref, pre_ref, post_ref,
                      w_in_ref, w_grp_ref, scale_ref, w_out_ref, o_ref, u_scr, c_a, c_b, *, seq):
    tm = x_ref.shape[1]
    width = scale_ref.shape[1]
    gdim = width // POOL_GROUPS
    ext = tm + 2 * POOL_HALO
    i = pl.program_id(1)
    xh, oah = _with_halo(xp_ref, x_ref, xn_ref), _with_halo(oap_ref, oa_ref, oan_ref)
    obh, gh = _with_halo(obp_ref, ob_ref, obn_ref), _with_halo(gp_ref, g_ref, gn_ref)
    blk = -(-ext // (BACK_ROW_BLOCKS * BF16_TILE_ROWS)) * BF16_TILE_ROWS
    edges = [min(k * blk, ext) for k in range(BACK_ROW_BLOCKS + 1)]
    xes = [_attn_mix(xh[a:b], oah[a:b], obh[a:b], gh[a:b], gate0_ref[0], post0_ref[...], w_out0_ref)
           for a, b in zip(edges[:-1], edges[1:])]
    xe = jnp.concatenate(xes, axis=0)
    x = xe[POOL_HALO:POOL_HALO + tm]
    he = jnp.concatenate([_mod_norm(v, pre_ref[...], scl_ref[0], shift_ref[0]).astype(BF16) for v in xes],
                         axis=0)
    t_ext = i * tm - POOL_HALO + lax.broadcasted_iota(jnp.int32, (ext, 1), 0)
    u_ext = _dot(he, w_in_ref[:, :width])
    u_scr[0:ext, :] = jnp.where((t_ext >= 0) & (t_ext < seq), u_ext, 0.0)
    u_scr[ext:ext + POOL_TAIL, :] = jnp.zeros((POOL_TAIL, width), F32)
    gate_act = _silu(_dot(he[POOL_HALO:POOL_HALO + tm], w_in_ref[:, width:]))
    t = i * tm + lax.broadcasted_iota(jnp.int32, (tm, 1), 0)
    m = jnp.zeros((tm, o_ref.shape[2]), F32)
    for g, window in enumerate(POOL_WINDOWS):
        half = window // 2
        assert half <= POOL_HALO
        cols = slice(g * gdim, (g + 1) * gdim)
        read = lambda e, n, cols=cols: u_scr[e:e + n, cols]
        span, n, bufs = 1, ext + POOL_TAIL, (c_a, c_b)
        while span < half:
            n -= SUBLANES
            dst = bufs[0]
            dst[0:n, :] = read(0, n) + read(span, n)
            read = lambda e, n, dst=dst: dst[e:e + n, :]
            span, bufs = 2 * span, bufs[::-1]
        win = read(POOL_HALO - half, tm) + read(POOL_HALO, tm)
        cnt = (jnp.minimum(t + half, seq) - jnp.maximum(t - half, 0)).astype(F32)
        pooled = win * (1.0 / cnt) - u_scr[POOL_HALO:POOL_HALO + tm, cols]
        mixed = _dot(pooled.astype(BF16), w_grp_ref[g]) * scale_ref[:, cols]
        y = (mixed * gate_act[:, cols]).astype(BF16)
        m = m + _dot(y, w_out_ref[cols, :])
    o_ref[0] = _post_residual(x, m, post_ref[...], gate_ref[0])


def _back_pool_layer(x, oa, ob, g, gate0, post0_g, w_out0, shift, scl, gate, pre_g, post_g,
                     w_in, w_grp, scale, w_out, tm):
    b, s, d = x.shape
    width = scale.shape[1]
    row = lambda i, j: (i, j, 0)
    per_b = lambda i, j: (i, 0, 0)
    const = lambda i, j: (0, 0)

    def with_halo(cols, halo_rows):
        nt, last = tm // halo_rows, s // halo_rows - 1
        return [pl.BlockSpec((1, halo_rows, cols), lambda i, j: (i, jnp.maximum(j * nt - 1, 0), 0)),
                pl.BlockSpec((1, tm, cols), row),
                pl.BlockSpec((1, halo_rows, cols), lambda i, j: (i, jnp.minimum((j + 1) * nt, last), 0))]

    scratch_rows = tm + 2 * POOL_HALO + POOL_TAIL
    return pl.pallas_call(
        functools.partial(_back_pool_kernel, seq=s),
        grid=(b, s // tm),
        in_specs=with_halo(d, POOL_HALO) + with_halo(A_WIDTH, BF16_TILE_ROWS)
        + with_halo(B_WIDTH, BF16_TILE_ROWS) + with_halo(MIX_WIDTH, BF16_TILE_ROWS)
        + [pl.BlockSpec((1, 1, d), per_b), pl.BlockSpec((1, d), const), pl.BlockSpec((MIX_WIDTH, d), const),
           pl.BlockSpec((1, 1, d), per_b), pl.BlockSpec((1, 1, d), per_b), pl.BlockSpec((1, 1, d), per_b),
           pl.BlockSpec((1, d), const), pl.BlockSpec((1, d), const), pl.BlockSpec((d, 2 * width), const),
           pl.BlockSpec(w_grp.shape, lambda i, j: (0, 0, 0)),
           pl.BlockSpec((1, width), const), pl.BlockSpec((width, d), const)],
        out_specs=pl.BlockSpec((1, tm, d), row),
        out_shape=jax.ShapeDtypeStruct((b, s, d), F32),
        scratch_shapes=[pltpu.VMEM((scratch_rows, width), F32),
                        pltpu.VMEM((scratch_rows, width // POOL_GROUPS), F32),
                        pltpu.VMEM((scratch_rows, width // POOL_GROUPS), F32)],
        compiler_params=_params("parallel", "parallel"),
        name="back_pool_layer",
    )(x, x, x, oa, oa, oa, ob, ob, ob, g, g, g, gate0, post0_g, w_out0,
      shift, scl, gate, pre_g, post_g, w_in, w_grp, scale, w_out)


def _trunk(x, mods, pre_norm, post_norm, attn_w_in, attn_q_norm, attn_k_norm, attn_w_out,
           pool_w_in, pool_w_grp, pool_scale, pool_w_out, tabs, bd, *, tm, tq, tk):
    depth = pre_norm.shape[0]
    d = x.shape[-1]

    def mod(l):
        return tuple(mods[l][:, None, j * d:(j + 1) * d] for j in range(3))

    for l in range(0, depth, 2):
        i = l // 2
        shift, scl, gate = mod(l)
        gq = jnp.tile(attn_q_norm[i], LANES // HEAD_DIM)[None, :]
        gk = jnp.tile(attn_k_norm[i], LANES // HEAD_DIM)[None, :]
        qa, ka, va, qb, kt, vb, g = _attn_front(x, shift, scl, pre_norm[l][None, :], attn_w_in[i], gq, gk,
                                                bd, tabs, tm)
        oa = _dilated_attn(qa, ka, va)
        ob = _gqa_flash(qb, kt, vb, tq, tk)
        back = (x, oa, ob, g, gate, post_norm[l][None, :], attn_w_out[i])
        if l + 1 < depth:
            shift1, scl1, gate1 = mod(l + 1)
            x = _back_pool_layer(*back, shift1, scl1, gate1, pre_norm[l + 1][None, :],
                                 post_norm[l + 1][None, :], pool_w_in[i], pool_w_grp[i],
                                 pool_scale[i][None, :], pool_w_out[i], tm)
        else:
            x = _attn_back(*back, tm)
    return x


def kernel(x_prompt, x_sample, c_prompt, c_sample, ada_w, ada_b, pre_norm, post_norm, attn_w_in, attn_q_norm, attn_k_norm, attn_w_out, pool_w_in, pool_w_grp, pool_scale, pool_w_out):
    nb_p = x_prompt.shape[0]
    mods = _ada_mod(jnp.concatenate([c_prompt, c_sample], axis=0), ada_w, ada_b)
    head_of_lane = jnp.arange(MXU_DIM) // HEAD_DIM
    bd = (head_of_lane[:, None] == head_of_lane[None, :]).astype(BF16)
    weights = (pre_norm, post_norm, attn_w_in.astype(BF16), attn_q_norm, attn_k_norm,
               attn_w_out.astype(BF16), pool_w_in.astype(BF16), pool_w_grp.astype(BF16), pool_scale,
               pool_w_out.astype(BF16))
    outs = []
    for x, sl in ((x_prompt, slice(0, nb_p)), (x_sample, slice(nb_p, None))):
        tabs = _make_tables(x.shape[1])
        outs.append(_trunk(x, mods[:, sl], *weights, tabs, bd, tm=512, tq=512, tk=512))
    return tuple(outs)
```

```python
import functools
import math

import jax
import jax.numpy as jnp
from jax import lax
from jax.experimental import pallas as pl
from jax.experimental.pallas import tpu as pltpu

HEAD_DIM = 64
A_HEADS = 8
B_HEADS = 8
B_KV_HEADS = 2
A_WIDTH = A_HEADS * HEAD_DIM
B_WIDTH = B_HEADS * HEAD_DIM
B_KV_WIDTH = B_KV_HEADS * HEAD_DIM
MIX_WIDTH = A_WIDTH + B_WIDTH
DILATED_PATTERNS = ((128, 1), (512, 4), (2048, 16))
ROPE_THETA = 500000.0
ROPE_DIM = HEAD_DIM // 4
AXIAL_THETA = 10000.0
GRID_W = 64
POOL_WINDOWS = (2, 4, 8, 16)
POOL_GROUPS = 4
NORM_EPS = 1e-6
NEG_INF = -1e30

LANES = 128
SUBLANES = 8
MXU_DIM = 256
FRONT_BLOCK_ROWS = MXU_DIM
VMEM_LIMIT_BYTES = 56 * 1024 * 1024
Q_SCALE = math.log2(math.e) / math.sqrt(HEAD_DIM)

BF16 = jnp.bfloat16
F32 = jnp.float32


def _params(*semantics):
    return pltpu.CompilerParams(dimension_semantics=semantics, vmem_limit_bytes=VMEM_LIMIT_BYTES)


def _dot(a, b):
    return jnp.dot(a, b, preferred_element_type=F32)


def _silu(x):
    return x * (1.0 / (1.0 + jnp.exp(-x)))


def _mod_norm(x, pre_g, scl, shift):
    ms = jnp.mean(x * x, axis=-1, keepdims=True)
    return x * lax.rsqrt(ms + NORM_EPS) * (pre_g * (1.0 + scl)) + shift


def _post_residual(x, m, post_g, gate):
    ms = jnp.mean(m * m, axis=-1, keepdims=True)
    return x + m * lax.rsqrt(ms + NORM_EPS) * (gate * post_g)


def _ada_kernel(c_ref, w_ref, b_ref, o_ref):
    a = _silu(c_ref[...]).astype(BF16)
    o_ref[0] = _dot(a, w_ref[0].astype(BF16)) + b_ref[0]


def _ada_mod(c_all, ada_w, ada_b):
    depth, d, d3 = ada_w.shape
    nb = c_all.shape[0]
    tn = d
    return pl.pallas_call(
        _ada_kernel,
        grid=(depth, d3 // tn),
        in_specs=[pl.BlockSpec((nb, d), lambda l, j: (0, 0)),
                  pl.BlockSpec((1, d, tn), lambda l, j: (l, 0, j)),
                  pl.BlockSpec((1, 1, tn), lambda l, j: (l, 0, j))],
        out_specs=pl.BlockSpec((1, nb, tn), lambda l, j: (l, 0, j)),
        out_shape=jax.ShapeDtypeStruct((depth, nb, d3), F32),
        compiler_params=_params("arbitrary", "arbitrary"),
        name="ada_mod",
    )(c_all, ada_w, ada_b.reshape(depth, 1, d3))


def _rope_tables(pos, rot_dim, theta, lane_in_block):
    h = rot_dim // 2
    f = jnp.where(lane_in_block >= 0, lane_in_block % h, 0)
    inv = theta ** (-(2.0 * f.astype(F32)) / rot_dim)
    ang = pos.astype(F32)[:, None] * inv[None, :]
    active = (lane_in_block >= 0)[None, :]
    lo = (lane_in_block < h)[None, :] & active
    hi = (lane_in_block >= h)[None, :] & active
    cos = jnp.where(active, jnp.cos(ang), 1.0)
    sin = jnp.sin(ang)
    return cos, jnp.where(lo, -sin, 0.0), jnp.where(hi, sin, 0.0)


def _make_tables(s):
    lane = jnp.arange(LANES) % HEAD_DIM
    pos = jnp.arange(s)
    a_tabs = _rope_tables(pos, ROPE_DIM, ROPE_THETA, jnp.where(lane < ROPE_DIM, lane, -1))
    half = HEAD_DIM // 2
    in_blk = lane % half
    row_t = _rope_tables(pos // GRID_W, half, AXIAL_THETA, in_blk)
    col_t = _rope_tables(pos % GRID_W, half, AXIAL_THETA, in_blk)
    first = (lane < half)[None, :]
    b_tabs = tuple(jnp.where(first, r, c) for r, c in zip(row_t, col_t))
    return a_tabs + b_tabs


def _apply_rope(x, cos, s_lo, s_hi, h):
    return x * cos + pltpu.roll(x, LANES - h, 1) * s_lo + pltpu.roll(x, h, 1) * s_hi


def _attn_front_kernel(x_ref, shift_ref, scl_ref, pre_ref, w_ref, gq_ref, gk_ref, bd_ref,
                       ac_ref, alo_ref, ahi_ref, bc_ref, blo_ref, bhi_ref,
                       qa_ref, ka_ref, va_ref, qb_ref, kt_ref, vb_ref, g_ref):
    lane = lax.broadcasted_iota(jnp.int32, (1, LANES), 1)

    def head_rms(xc, g):
        n = xc.shape[1]
        bd = bd_ref[0:n, 0:n]
        sq = xc * xc
        hi = sq.astype(BF16)
        lo = (sq - hi.astype(F32)).astype(BF16)
        ss = _dot(hi, bd) + _dot(lo, bd)
        g = jnp.concatenate([g] * (n // LANES), axis=1)
        return xc * lax.rsqrt(ss * (1.0 / HEAD_DIM) + NORM_EPS) * g

    tm = x_ref.shape[1]
    blk = FRONT_BLOCK_ROWS
    for r in range(tm // blk):
        rows = slice(r * blk, (r + 1) * blk)
        h = _mod_norm(x_ref[0, rows, :], pre_ref[...], scl_ref[0], shift_ref[0]).astype(BF16)
        a_tabs = (ac_ref[rows, :], alo_ref[rows, :], ahi_ref[rows, :])
        b_tabs = (bc_ref[rows, :], blo_ref[rows, :], bhi_ref[rows, :])
        off = 0
        qa = _dot(h, w_ref[:, off:off + A_WIDTH]); off += A_WIDTH
        for c in range(A_WIDTH // LANES):
            col = _apply_rope(qa[:, c * LANES:(c + 1) * LANES], *a_tabs, ROPE_DIM // 2)
            qa_ref[0, rows, c * LANES:(c + 1) * LANES] = (col * Q_SCALE).astype(BF16)
        ka = _dot(h, w_ref[:, off:off + A_WIDTH]); off += A_WIDTH
        for c in range(A_WIDTH // LANES):
            col = _apply_rope(ka[:, c * LANES:(c + 1) * LANES], *a_tabs, ROPE_DIM // 2)
            ka_ref[0, rows, c * LANES:(c + 1) * LANES] = col.astype(BF16)
        va_ref[0, rows, :] = _dot(h, w_ref[:, off:off + A_WIDTH]).astype(BF16); off += A_WIDTH

        qb = _dot(h, w_ref[:, off:off + B_WIDTH]); off += B_WIDTH
        wide = bd_ref.shape[0]
        for c2 in range(B_WIDTH // wide):
            blkq = head_rms(qb[:, c2 * wide:(c2 + 1) * wide], gq_ref[...])
            for c in range(wide // LANES):
                col = _apply_rope(blkq[:, c * LANES:(c + 1) * LANES], *b_tabs, HEAD_DIM // 4)
                dst = c2 * wide + c * LANES
                qb_ref[0, rows, dst:dst + LANES] = (col * Q_SCALE).astype(BF16)

        kb = _dot(h, w_ref[:, off:off + B_KV_WIDTH]); off += B_KV_WIDTH
        kb = _apply_rope(head_rms(kb, gk_ref[...]), *b_tabs, HEAD_DIM // 4)
        kt = kb.T
        for j in range(B_KV_HEADS):
            kj = kt[j * HEAD_DIM:(j + 1) * HEAD_DIM].astype(BF16)
            kt_ref[0, j, :, rows] = jnp.concatenate([kj, kj], axis=0)

        vb = _dot(h, w_ref[:, off:off + B_KV_WIDTH]); off += B_KV_WIDTH
        ones_col = jnp.where(lane == HEAD_DIM, 1.0, 0.0)
        for j in range(B_KV_HEADS):
            vj = vb if j == 0 else pltpu.roll(vb, HEAD_DIM, 1)
            vb_ref[0, j, rows, :] = jnp.where(lane < HEAD_DIM, vj, ones_col).astype(BF16)

        g_ref[0, rows, :] = _silu(_dot(h, w_ref[:, off:off + MIX_WIDTH])).astype(BF16)


def _attn_front(x, shift, scl, pre_g, w_in, gq, gk, bd, tabs, tm):
    b, s, d = x.shape
    n_in = w_in.shape[1]
    row = lambda i, j: (i, j, 0)
    per_b = lambda i, j: (i, 0, 0)
    const = lambda i, j: (0, 0)
    tab_spec = pl.BlockSpec((tm, LANES), lambda i, j: (j, 0))
    wide = lambda w: pl.BlockSpec((1, tm, w), row)
    return pl.pallas_call(
        _attn_front_kernel,
        grid=(b, s // tm),
        in_specs=[wide(d), pl.BlockSpec((1, 1, d), per_b), pl.BlockSpec((1, 1, d), per_b),
                  pl.BlockSpec((1, d), const), pl.BlockSpec((d, n_in), const),
                  pl.BlockSpec((1, LANES), const), pl.BlockSpec((1, LANES), const),
                  pl.BlockSpec(bd.shape, const)] + [tab_spec] * 6,
        out_specs=[wide(A_WIDTH), wide(A_WIDTH), wide(A_WIDTH), wide(B_WIDTH),
                   pl.BlockSpec((1, B_KV_HEADS, LANES, tm), lambda i, j: (i, 0, 0, j)),
                   pl.BlockSpec((1, B_KV_HEADS, tm, LANES), lambda i, j: (i, 0, j, 0)),
                   wide(MIX_WIDTH)],
        out_shape=[jax.ShapeDtypeStruct((b, s, A_WIDTH), BF16)] * 3
        + [jax.ShapeDtypeStruct((b, s, B_WIDTH), BF16),
           jax.ShapeDtypeStruct((b, B_KV_HEADS, LANES, s), BF16),
           jax.ShapeDtypeStruct((b, B_KV_HEADS, s, LANES), BF16),
           jax.ShapeDtypeStruct((b, s, MIX_WIDTH), BF16)],
        compiler_params=_params("parallel", "parallel"),
        name="attn_front",
    )(x, shift, scl, pre_g, w_in, gq, gk, bd, *tabs)


def _gqa_kernel(q_ref, kt_ref, v_ref, o_ref, *, tk):
    tq = q_ref.shape[1]
    s = kt_ref.shape[3]
    lane = lax.broadcasted_iota(jnp.int32, (1, LANES), 1)
    first = lane < HEAD_DIM
    zero = jnp.zeros((), BF16)
    parts = []
    for c in range(2):
        qc = q_ref[0, :, c * LANES:(c + 1) * LANES]
        parts += [jnp.where(first, qc, zero), jnp.where(first, zero, qc)]
    qs = jnp.concatenate(parts, axis=0)
    rows = qs.shape[0]

    def step(i, carry):
        m, acc = carry
        k0 = pl.multiple_of(i * tk, tk)
        sc = _dot(qs, kt_ref[0, 0, :, pl.ds(k0, tk)])
        m_new = jnp.maximum(m, jnp.max(sc, axis=1, keepdims=True))
        p = jnp.exp2(sc - m_new).astype(BF16)
        acc = acc * jnp.exp2(m - m_new) + _dot(p, v_ref[0, 0, pl.ds(k0, tk), :])
        return m_new, acc

    m0 = jnp.full((rows, 1), NEG_INF, F32)
    acc0 = jnp.zeros((rows, LANES), F32)
    _, acc = lax.fori_loop(0, s // tk, step, (m0, acc0), unroll=True)
    out = acc / acc[:, HEAD_DIM:HEAD_DIM + 1]
    for c in range(2):
        even = out[(2 * c) * tq:(2 * c + 1) * tq]
        odd = pltpu.roll(out[(2 * c + 1) * tq:(2 * c + 2) * tq], HEAD_DIM, 1)
        o_ref[0, :, c * LANES:(c + 1) * LANES] = jnp.where(first, even, odd).astype(BF16)


def _gqa_flash(qb, kt, vb, tq, tk):
    b, s, _ = qb.shape
    group_w = B_WIDTH // B_KV_HEADS
    return pl.pallas_call(
        functools.partial(_gqa_kernel, tk=tk),
        grid=(b, B_KV_HEADS, s // tq),
        in_specs=[pl.BlockSpec((1, tq, group_w), lambda i, j, t: (i, t, j)),
                  pl.BlockSpec((1, 1, LANES, s), lambda i, j, t: (i, j, 0, 0)),
                  pl.BlockSpec((1, 1, s, LANES), lambda i, j, t: (i, j, 0, 0))],
        out_specs=pl.BlockSpec((1, tq, group_w), lambda i, j, t: (i, t, j)),
        out_shape=jax.ShapeDtypeStruct((b, s, B_WIDTH), BF16),
        compiler_params=_params("parallel", "parallel", "arbitrary"),
        name="gqa_flash",
    )(qb, kt, vb)


A_CHUNK = 128
A_RADIUS = 64
A_GROUP = 32
A_PERM_BLOCK = MXU_DIM
_A_SORTED_DILATIONS = tuple(d for _, d in DILATED_PATTERNS if d != 1)


def _class_perm(dil):
    i = jnp.arange(A_PERM_BLOCK)
    m = A_PERM_BLOCK // dil
    src = (i % m) * dil + i // m
    return (src[:, None] == i[None, :]).astype(BF16)


def _dilated_kernel(start_ref, q_ref, k_ref, v_ref, p4_ref, p16_ref, o_ref, q0d, q1d, kd, vd, bias,
                    st_a, st_b, tile):
    perm_refs = dict(zip(_A_SORTED_DILATIONS, (p4_ref, p16_ref)))
    s = q_ref.shape[1]
    n_chunks = s // A_CHUNK
    kw = 2 * A_CHUNK
    lane = lax.broadcasted_iota(jnp.int32, (1, LANES), 1)
    first = lane < HEAD_DIM

    tail = kw - A_RADIUS
    kd[0:A_RADIUS, :] = jnp.zeros((A_RADIUS, LANES), BF16)
    kd[A_RADIUS + s:A_RADIUS + s + tail, :] = jnp.zeros((tail, LANES), BF16)
    vd[0:A_RADIUS, 0:LANES] = jnp.zeros((A_RADIUS, LANES), BF16)
    vd[A_RADIUS + s:A_RADIUS + s + tail, 0:LANES] = jnp.zeros((tail, LANES), BF16)
    vd[:, LANES:2 * LANES] = jnp.ones((vd.shape[0], LANES), BF16)

    ri = lax.broadcasted_iota(jnp.int32, (A_CHUNK, kw), 0)
    ci = lax.broadcasted_iota(jnp.int32, (A_CHUNK, kw), 1)
    band = (ci >= ri) & (ci <= ri + 2 * A_RADIUS)
    for at_start in (0, 1):
        for at_end in (0, 1):
            ok = band
            if at_start:
                ok = ok & (ci >= A_RADIUS)
            if at_end:
                ok = ok & (ci < A_RADIUS + A_CHUNK)
            bias[at_start + 2 * at_end] = jnp.where(ok, 0.0, NEG_INF)

    order = sorted(DILATED_PATTERNS, key=lambda wd: -wd[1])
    states = (st_a, st_b)
    parent_dil = None
    for p, (window, dil) in enumerate(order):
        assert window // (2 * dil) == A_RADIUS
        cls_len = s // dil
        cls_chunks = cls_len // A_CHUNK
        final = p == len(order) - 1
        assert (dil == 1) == final
        src_state, dst_state = states[(p + 1) % 2], states[p % 2]

        if dil == 1:
            qv = q_ref[0]
            q0d[...] = jnp.where(first, qv, jnp.zeros((), BF16))
            q1d[...] = jnp.where(first, jnp.zeros((), BF16), qv)
            kd[A_RADIUS:A_RADIUS + s, :] = k_ref[0]
            vd[A_RADIUS:A_RADIUS + s, 0:LANES] = v_ref[0]
        else:
            perm = perm_refs[dil][...]
            piece = A_PERM_BLOCK // dil

            def sort_block(bk, carry, dil=dil, cls_len=cls_len, perm=perm, piece=piece):
                r0 = pl.multiple_of(bk * A_PERM_BLOCK, A_PERM_BLOCK)
                qv = q_ref[0, pl.ds(r0, A_PERM_BLOCK), :]
                zero = jnp.zeros((), BF16)
                qm = jnp.concatenate([jnp.where(first, qv, zero), jnp.where(first, zero, qv)], axis=1)
                kv = jnp.concatenate([k_ref[0, pl.ds(r0, A_PERM_BLOCK), :],
                                      v_ref[0, pl.ds(r0, A_PERM_BLOCK), :]], axis=1)
                yq = _dot(perm, qm).astype(BF16)
                ykv = _dot(perm, kv).astype(BF16)
                for r in range(dil):
                    src = slice(r * piece, (r + 1) * piece)
                    row = pl.multiple_of(r * cls_len + bk * piece, piece)
                    q0d[pl.ds(row, piece), :] = yq[src, :LANES]
                    q1d[pl.ds(row, piece), :] = yq[src, LANES:]
                    kd[pl.ds(A_RADIUS + row, piece), :] = ykv[src, :LANES]
                    vd[pl.ds(A_RADIUS + row, piece), 0:LANES] = ykv[src, LANES:]
                return carry

            lax.fori_loop(0, s // A_PERM_BLOCK, sort_block, 0, unroll=True)

        def chunk(c, slot, dil=dil, cls_chunks=cls_chunks, final=final, parent_dil=parent_dil,
                  src_state=src_state, dst_state=dst_state):
            r0 = pl.multiple_of(c * A_CHUNK, A_CHUNK)
            qq = jnp.concatenate([q0d[pl.ds(r0, A_CHUNK), :], q1d[pl.ds(r0, A_CHUNK), :]], axis=0)
            kwin = kd[pl.ds(r0, kw), :]
            vwin = vd[pl.ds(r0, kw), :]
            sc = lax.dot_general(qq, kwin, (((1,), (1,)), ((), ())), preferred_element_type=F32)
            in_cls = c % cls_chunks
            mask = bias[jnp.where(in_cls == 0, 1, 0) + jnp.where(in_cls == cls_chunks - 1, 2, 0)]
            sc = sc + jnp.concatenate([mask, mask], axis=0)
            m = jnp.max(sc, axis=1, keepdims=True)
            pr = jnp.exp2(sc - m).astype(BF16)
            pv = _dot(pr, vwin)
            pick = lambda x: jnp.where(first, x[:A_CHUNK], x[A_CHUNK:])
            num, den = pick(pv[:, :LANES]), pick(pv[:, LANES:])
            top = pick(jnp.broadcast_to(m, (2 * A_CHUNK, LANES)))
            if parent_dil is not None:
                fan = parent_dil // dil
                piece = A_CHUNK // fan
                cls = c // cls_chunks
                for a in range(fan):
                    row = pl.multiple_of((cls + dil * a) * (s // parent_dil) + in_cls * piece, piece)
                    for k in range(3):
                        tile[slot, k, pl.ds(a, piece, stride=fan), :] = src_state[k, pl.ds(row, piece), :]
                top_p = tile[slot, 1]
                top_new = jnp.maximum(top, top_p)
                w_c, w_p = jnp.exp2(top - top_new), jnp.exp2(top_p - top_new)
                num = w_c * num + w_p * tile[slot, 0]
                den = w_c * den + w_p * tile[slot, 2]
                top = top_new
            if final:
                o_ref[0, pl.ds(r0, A_CHUNK), :] = (num * (1.0 / den)).astype(BF16)
            else:
                dst_state[0, pl.ds(r0, A_CHUNK), :] = num
                dst_state[1, pl.ds(r0, A_CHUNK), :] = top
                dst_state[2, pl.ds(r0, A_CHUNK), :] = den

        group = min(A_GROUP, n_chunks)
        assert n_chunks % group == 0

        def chunk_group(g, carry, chunk=chunk, group=group):
            for slot in range(group):
                chunk(g * group + slot, slot)
            return carry

        lax.fori_loop(start_ref[0], start_ref[0] + n_chunks // group, chunk_group, 0)
        parent_dil = dil


def _dilated_attn(qa, ka, va):
    b, s, w = qa.shape
    spec = pl.BlockSpec((1, s, LANES), lambda i, j: (i, 0, j))
    pad_rows = s + 2 * A_CHUNK
    perm_spec = pl.BlockSpec((A_PERM_BLOCK, A_PERM_BLOCK), lambda i, j: (0, 0))
    perms = [_class_perm(d) for d in _A_SORTED_DILATIONS]
    return pl.pallas_call(
        _dilated_kernel,
        grid=(b, w // LANES),
        in_specs=[pl.BlockSpec(memory_space=pltpu.SMEM), spec, spec, spec] + [perm_spec] * len(perms),
        out_specs=spec,
        out_shape=jax.ShapeDtypeStruct((b, s, w), BF16),
        scratch_shapes=[pltpu.VMEM((s, LANES), BF16),
                        pltpu.VMEM((s, LANES), BF16),
                        pltpu.VMEM((pad_rows, LANES), BF16),
                        pltpu.VMEM((pad_rows, 2 * LANES), BF16),
                        pltpu.VMEM((4, A_CHUNK, 2 * A_CHUNK), F32),
                        pltpu.VMEM((3, s, LANES), F32),
                        pltpu.VMEM((3, s, LANES), F32),
                        pltpu.VMEM((A_GROUP, 3, A_CHUNK, LANES), F32)],
        compiler_params=_params("parallel", "parallel"),
        name="dilated_attn",
    )(jnp.zeros((1,), jnp.int32), qa, ka, va, *perms)


def _attn_mix(x, oa, ob, g, gate, post_g, w_ref):
    ya = (oa.astype(F32) * g[:, :A_WIDTH].astype(F32)).astype(BF16)
    yb = (ob.astype(F32) * g[:, A_WIDTH:].astype(F32)).astype(BF16)
    m = _dot(ya, w_ref[:A_WIDTH, :]) + _dot(yb, w_ref[A_WIDTH:, :])
    return _post_residual(x, m, post_g, gate)


def _attn_back_kernel(x_ref, oa_ref, ob_ref, g_ref, gate_ref, post_ref, w_ref, o_ref):
    o_ref[0] = _attn_mix(x_ref[0], oa_ref[0], ob_ref[0], g_ref[0], gate_ref[0], post_ref[...], w_ref)


def _attn_back(x, oa, ob, g, gate, post_g, w_out, tm):
    b, s, d = x.shape
    row = lambda i, j: (i, j, 0)
    per_b = lambda i, j: (i, 0, 0)
    const = lambda i, j: (0, 0)
    return pl.pallas_call(
        _attn_back_kernel,
        grid=(b, s // tm),
        in_specs=[pl.BlockSpec((1, tm, d), row), pl.BlockSpec((1, tm, A_WIDTH), row),
                  pl.BlockSpec((1, tm, B_WIDTH), row), pl.BlockSpec((1, tm, MIX_WIDTH), row),
                  pl.BlockSpec((1, 1, d), per_b), pl.BlockSpec((1, d), const),
                  pl.BlockSpec((MIX_WIDTH, d), const)],
        out_specs=pl.BlockSpec((1, tm, d), row),
        out_shape=jax.ShapeDtypeStruct((b, s, d), F32),
        compiler_params=_params("parallel", "parallel"),
        name="attn_back",
    )(x, oa, ob, g, gate, post_g, w_out)


POOL_HALO = SUBLANES
POOL_TAIL = 2 * SUBLANES
BF16_TILE_ROWS = 2 * SUBLANES
BACK_ROW_BLOCKS = 3


def _with_halo(prev_ref, main_ref, next_ref):
    prev = prev_ref[0].astype(F32)
    return jnp.concatenate([prev[prev.shape[0] - POOL_HALO:], main_ref[0].astype(F32),
                            next_ref[0].astype(F32)[:POOL_HALO]], axis=0)


def _back_pool_kernel(xp_ref, x_ref, xn_ref, oap_ref, oa_ref, oan_ref, obp_ref, ob_ref, obn_ref,
                      gp_ref, g_ref, gn_ref, gate0_ref, post0_ref, w_out0_ref,
                      shift_ref, scl_ref, gate_ref, pre_ref, post_ref,
                      w_in_ref, w_grp_ref, scale_ref, w_out_ref, o_ref, u_scr, c_a, c_b, *, seq):
    tm = x_ref.shape[1]
    width = scale_ref.shape[1]
    gdim = width // POOL_GROUPS
    ext = tm + 2 * POOL_HALO
    i = pl.program_id(1)
    xh, oah = _with_halo(xp_ref, x_ref, xn_ref), _with_halo(oap_ref, oa_ref, oan_ref)
    obh, gh = _with_halo(obp_ref, ob_ref, obn_ref), _with_halo(gp_ref, g_ref, gn_ref)
    blk = -(-ext // (BACK_ROW_BLOCKS * BF16_TILE_ROWS)) * BF16_TILE_ROWS
    edges = [min(k * blk, ext) for k in range(BACK_ROW_BLOCKS + 1)]
    xes = [_attn_mix(xh[a:b], oah[a:b], obh[a:b], gh[a:b], gate0_ref[0], post0_ref[...], w_out0_ref)
           for a, b in zip(edges[:-1], edges[1:])]
    xe = jnp.concatenate(xes, axis=0)
    x = xe[POOL_HALO:POOL_HALO + tm]
    he = jnp.concatenate([_mod_norm(v, pre_ref[...], scl_ref[0], shift_ref[0]).astype(BF16) for v in xes],
                         axis=0)
    t_ext = i * tm - POOL_HALO + lax.broadcasted_iota(jnp.int32, (ext, 1), 0)
    u_ext = _dot(he, w_in_ref[:, :width])
    u_scr[0:ext, :] = jnp.where((t_ext >= 0) & (t_ext < seq), u_ext, 0.0)
    u_scr[ext:ext + POOL_TAIL, :] = jnp.zeros((POOL_TAIL, width), F32)
    gate_act = _silu(_dot(he[POOL_HALO:POOL_HALO + tm], w_in_ref[:, width:]))
    t = i * tm + lax.broadcasted_iota(jnp.int32, (tm, 1), 0)
    m = jnp.zeros((tm, o_ref.shape[2]), F32)
    for g, window in enumerate(POOL_WINDOWS):
        half = window // 2
        assert half <= POOL_HALO
        cols = slice(g * gdim, (g + 1) * gdim)
        read = lambda e, n, cols=cols: u_scr[e:e + n, cols]
        span, n, bufs = 1, ext + POOL_TAIL, (c_a, c_b)
        while span < half:
            n -= SUBLANES
            dst = bufs[0]
            dst[0:n, :] = read(0, n) + read(span, n)
            read = lambda e, n, dst=dst: dst[e:e + n, :]
            span, bufs = 2 * span, bufs[::-1]
        win = read(POOL_HALO - half, tm) + read(POOL_HALO, tm)
        cnt = (jnp.minimum(t + half, seq) - jnp.maximum(t - half, 0)).astype(F32)
        pooled = win * (1.0 / cnt) - u_scr[POOL_HALO:POOL_HALO + tm, cols]
        mixed = _dot(pooled.astype(BF16), w_grp_ref[g]) * scale_ref[:, cols]
        y = (mixed * gate_act[:, cols]).astype(BF16)
        m = m + _dot(y, w_out_ref[cols, :])
    o_ref[0] = _post_residual(x, m, post_ref[...], gate_ref[0])


def _back_pool_layer(x, oa, ob, g, gate0, post0_g, w_out0, shift, scl, gate, pre_g, post_g,
                     w_in, w_grp, scale, w_out, tm):
    b, s, d = x.shape
    width = scale.shape[1]
    row = lambda i, j: (i, j, 0)
    per_b = lambda i, j: (i, 0, 0)
    const = lambda i, j: (0, 0)

    def with_halo(cols, halo_rows):
        nt, last = tm // halo_rows, s // halo_rows - 1
        return [pl.BlockSpec((1, halo_rows, cols), lambda i, j: (i, jnp.maximum(j * nt - 1, 0), 0)),
                pl.BlockSpec((1, tm, cols), row),
                pl.BlockSpec((1, halo_rows, cols), lambda i, j: (i, jnp.minimum((j + 1) * nt, last), 0))]

    scratch_rows = tm + 2 * POOL_HALO + POOL_TAIL
    return pl.pallas_call(
        functools.partial(_back_pool_kernel, seq=s),
        grid=(b, s // tm),
        in_specs=with_halo(d, POOL_HALO) + with_halo(A_WIDTH, BF16_TILE_ROWS)
        + with_halo(B_WIDTH, BF16_TILE_ROWS) + with_halo(MIX_WIDTH, BF16_TILE_ROWS)
        + [pl.BlockSpec((1, 1, d), per_b), pl.BlockSpec((1, d), const), pl.BlockSpec((MIX_WIDTH, d), const),
           pl.BlockSpec((1, 1, d), per_b), pl.BlockSpec((1, 1, d), per_b), pl.BlockSpec((1, 1, d), per_b),
           pl.BlockSpec((1, d), const), pl.BlockSpec((1, d), const), pl.BlockSpec((d, 2 * width), const),
           pl.BlockSpec(w_grp.shape, lambda i, j: (0, 0, 0)),
           pl.BlockSpec((1, width), const), pl.BlockSpec((width, d), const)],
        out_specs=pl.BlockSpec((1, tm, d), row),
        out_shape=jax.ShapeDtypeStruct((b, s, d), F32),
        scratch_shapes=[pltpu.VMEM((scratch_rows, width), F32),
                        pltpu.VMEM((scratch_rows, width // POOL_GROUPS), F32),
                        pltpu.VMEM((scratch_rows, width // POOL_GROUPS), F32)],
        compiler_params=_params("parallel", "parallel"),
        name="back_pool_layer",
    )(x, x, x, oa, oa, oa, ob, ob, ob, g, g, g, gate0, post0_g, w_out0,
      shift, scl, gate, pre_g, post_g, w_in, w_grp, scale, w_out)


def _trunk(x, mods, pre_norm, post_norm, attn_w_in, attn_q_norm, attn_k_norm, attn_w_out,
           pool_w_in, pool_w_grp, pool_scale, pool_w_out, tabs, bd, *, tm, tm_front, tq, tk):
    depth = pre_norm.shape[0]
    d = x.shape[-1]

    def mod(l):
        return tuple(mods[l][:, None, j * d:(j + 1) * d] for j in range(3))

    for l in range(0, depth, 2):
        i = l // 2
        shift, scl, gate = mod(l)
        gq = jnp.tile(attn_q_norm[i], LANES // HEAD_DIM)[None, :]
        gk = jnp.tile(attn_k_norm[i], LANES // HEAD_DIM)[None, :]
        qa, ka, va, qb, kt, vb, g = _attn_front(x, shift, scl, pre_norm[l][None, :], attn_w_in[i], gq, gk,
                                                bd, tabs, tm_front)
        oa = _dilated_attn(qa, ka, va)
        ob = _gqa_flash(qb, kt, vb, tq, tk)
        back = (x, oa, ob, g, gate, post_norm[l][None, :], attn_w_out[i])
        if l + 1 < depth:
            shift1, scl1, gate1 = mod(l + 1)
            x = _back_pool_layer(*back, shift1, scl1, gate1, pre_norm[l + 1][None, :],
                                 post_norm[l + 1][None, :], pool_w_in[i], pool_w_grp[i],
                                 pool_scale[i][None, :], pool_w_out[i], tm)
        else:
            x = _attn_back(*back, tm)
    return x


def kernel(x_prompt, x_sample, c_prompt, c_sample, ada_w, ada_b, pre_norm, post_norm, attn_w_in, attn_q_norm, attn_k_norm, attn_w_out, pool_w_in, pool_w_grp, pool_scale, pool_w_out):
    nb_p = x_prompt.shape[0]
    mods = _ada_mod(jnp.concatenate([c_prompt, c_sample], axis=0), ada_w, ada_b)
    head_of_lane = jnp.arange(MXU_DIM) // HEAD_DIM
    bd = (head_of_lane[:, None] == head_of_lane[None, :]).astype(BF16)
    weights = (pre_norm, post_norm, attn_w_in.astype(BF16), attn_q_norm, attn_k_norm,
               attn_w_out.astype(BF16), pool_w_in.astype(BF16), pool_w_grp.astype(BF16), pool_scale,
               pool_w_out.astype(BF16))
    outs = []
    for x, sl in ((x_prompt, slice(0, nb_p)), (x_sample, slice(nb_p, None))):
        tabs = _make_tables(x.shape[1])
        outs.append(_trunk(x, mods[:, sl], *weights, tabs, bd, tm=512, tm_front=1024, tq=512, tk=512))
    return tuple(outs)
```
